```python
import jax, jax.numpy as jnp
from jax import lax
import numpy as np

D_MODEL = 1024
BATCH = 8
SEQ = 2048
DEPTH = 1
DEC_BATCH = 128
DEC_SEQ = 1
PAST_LEN = 16384
PAGE_SIZE = 128

D_CONV = D_MODEL // 2
CONV_W = 31
D_GMLP = D_MODEL // 2
GMLP_HEADS = 8
GMLP_HEAD_DIM = D_GMLP // GMLP_HEADS
CHUNK = 128
IN_W = 2 * D_CONV + 2 * D_GMLP + 2 * D_MODEL
PEER_HEADS = 8
N_KEYS = 128
N_EXPERTS = N_KEYS * N_KEYS
D_KEY = 256
TOPK = 16
PEER_BLOCK = 128
EPS = 1e-6

kernel_name = "hybrid_conv_gmlp_peer_adaln_step"


def rmsnorm(x, g):
    xf = x.astype(jnp.float32)
    y = xf * lax.rsqrt(jnp.mean(xf * xf, axis=-1, keepdims=True) + EPS)
    return (y * g.astype(jnp.float32)).astype(x.dtype)


def layernorm(x, g, b):
    xf = x.astype(jnp.float32)
    mu = jnp.mean(xf, axis=-1, keepdims=True)
    var = jnp.mean(jnp.square(xf - mu), axis=-1, keepdims=True)
    y = (xf - mu) * lax.rsqrt(var + EPS)
    return (y * g.astype(jnp.float32) + b.astype(jnp.float32)).astype(x.dtype)


def causal_dwconv(full, w, b):
    y = lax.conv_general_dilated(full, w[:, None, :], window_strides=(1,), padding="VALID",
                                 dimension_numbers=("NWC", "WIO", "NWC"),
                                 feature_group_count=full.shape[-1])
    return y + b


def chunk_spatial_gate(v, w_s, b_s):
    n, s, _ = v.shape
    L = min(CHUNK, s)
    vh = v.reshape(n, s // L, L, GMLP_HEADS, GMLP_HEAD_DIM)
    mask = jnp.tril(jnp.ones((L, L), dtype=bool))
    wm = jnp.where(mask[None], w_s[:, :L, :L], jnp.zeros((), w_s.dtype))
    sg = jnp.einsum("hij,ncjhd->ncihd", wm, vh) + b_s[:, :L].T[None, None, :, :, None]
    return sg.reshape(n, s, D_GMLP)


def peer_block(xb, w_q, k1, k2, u_tab, v_tab):
    tb = xb.shape[0]
    half = D_KEY // 2
    q = (xb @ w_q).reshape(tb, PEER_HEADS, D_KEY)
    s1 = jnp.einsum("thd,nd->thn", q[..., :half], k1)
    s2 = jnp.einsum("thd,nd->thn", q[..., half:], k2)
    sc1, i1 = lax.top_k(s1, TOPK)
    sc2, i2 = lax.top_k(s2, TOPK)
    cand = (sc1[..., :, None] + sc2[..., None, :]).reshape(tb, PEER_HEADS, TOPK * TOPK)
    cid = (i1[..., :, None] * N_KEYS + i2[..., None, :]).reshape(tb, PEER_HEADS, TOPK * TOPK)
    top, pos = lax.top_k(cand, TOPK)
    eid = jnp.take_along_axis(cid, pos, axis=-1).reshape(tb, PEER_HEADS * TOPK)
    g = jax.nn.softmax(top.astype(jnp.float32), axis=-1).astype(xb.dtype).reshape(tb, PEER_HEADS * TOPK)
    act = jax.nn.gelu(jnp.einsum("tnd,td->tn", u_tab[eid], xb), approximate=False)
    return jnp.einsum("tn,tnd->td", g * act, v_tab[eid])


def peer(xn, w_q, k1, k2, u_tab, v_tab):
    n, s, d = xn.shape
    tot = n * s
    nb = -(-tot // PEER_BLOCK)
    t = jnp.pad(xn.reshape(tot, d), ((0, nb * PEER_BLOCK - tot), (0, 0))).reshape(nb, PEER_BLOCK, d)
    out = lax.map(lambda xb: peer_block(xb, w_q, k1, k2, u_tab, v_tab), t)
    return out.reshape(nb * PEER_BLOCK, d)[:tot].reshape(n, s, d)


def layer(x, c, hist, w_ada, b_ada, g_mix, w_in, w_dw, b_dw, g_cn, b_cn, w_conv_out,
          g_v, b_v, w_s, b_s, w_gmlp_out, w_out, g_ffn, w_q, k1, k2, u_tab, v_tab):
    mod = jax.nn.silu(c) @ w_ada + b_ada
    sh1, sc1, gt1, sh2, sc2, gt2 = [m[:, None, :] for m in jnp.split(mod, 6, axis=-1)]
    n = rmsnorm(x, g_mix) * (1 + sc1) + sh1
    z = n @ w_in
    a_lin, a_gate, u, v, g_a, g_b = jnp.split(
        z, [D_CONV, 2 * D_CONV, 2 * D_CONV + D_GMLP, 2 * D_CONV + 2 * D_GMLP,
            2 * D_CONV + 2 * D_GMLP + D_MODEL], axis=-1)
    a = a_lin * jax.nn.sigmoid(a_gate)
    full = jnp.concatenate([hist, a], axis=1)
    new_hist = full[:, full.shape[1] - (CONV_W - 1):]
    a = jax.nn.silu(layernorm(causal_dwconv(full, w_dw, b_dw), g_cn, b_cn)) @ w_conv_out
    v = layernorm(v, g_v, b_v)
    b_br = (u * chunk_spatial_gate(v, w_s, b_s)) @ w_gmlp_out
    m = jax.nn.sigmoid(g_a) * a + jax.nn.sigmoid(g_b) * b_br
    x = x + gt1 * (m @ w_out)
    n2 = rmsnorm(x, g_ffn) * (1 + sc2) + sh2
    x = x + gt2 * peer(n2, w_q, k1, k2, u_tab, v_tab)
    return x, new_hist, v


def setup_inputs(seed: int = 0) -> dict:
    key = jax.random.key(seed)
    ks = jax.random.split(key, 32)

    def nrm(k, shape, scale):
        return scale * jax.random.normal(k, shape, jnp.float32)

    L = DEPTH
    return {
        "x_prompt": nrm(ks[0], (BATCH, SEQ, D_MODEL), 1.0),
        "x_sample": nrm(ks[1], (DEC_BATCH, DEC_SEQ, D_MODEL), 1.0),
        "state_conv": nrm(ks[2], (DEPTH, DEC_BATCH, CONV_W - 1, D_CONV), 0.5),
        "c_prompt": nrm(ks[3], (BATCH, D_MODEL), 1.0),
        "c_sample": nrm(ks[4], (DEC_BATCH, D_MODEL), 1.0),
        "w_ada": nrm(ks[5], (L, D_MODEL, 6 * D_MODEL), 0.5 * D_MODEL ** -0.5),
        "b_ada": nrm(ks[6], (L, 6 * D_MODEL), 0.02),
        "g_mix": 1.0 + nrm(ks[7], (L, D_MODEL), 0.05),
        "w_in": nrm(ks[8], (L, D_MODEL, IN_W), D_MODEL ** -0.5),
        "w_dw": nrm(ks[9], (L, CONV_W, D_CONV), CONV_W ** -0.5),
        "b_dw": nrm(ks[10], (L, D_CONV), 0.02),
        "g_cn": 1.0 + nrm(ks[11], (L, D_CONV), 0.05),
        "b_cn": nrm(ks[12], (L, D_CONV), 0.02),
        "w_conv_out": nrm(ks[13], (L, D_CONV, D_MODEL), D_CONV ** -0.5),
        "g_v": 1.0 + nrm(ks[14], (L, D_GMLP), 0.05),
        "b_v": nrm(ks[15], (L, D_GMLP), 0.02),
        "w_s": nrm(ks[16], (L, GMLP_HEADS, CHUNK, CHUNK), CHUNK ** -0.5),
        "b_s": 1.0 + nrm(ks[17], (L, GMLP_HEADS, CHUNK), 0.1),
        "w_gmlp_out": nrm(ks[18], (L, D_GMLP, D_MODEL), D_GMLP ** -0.5),
        "w_out": nrm(ks[19], (L, D_MODEL, D_MODEL), D_MODEL ** -0.5),
        "g_ffn": 1.0 + nrm(ks[20], (L, D_MODEL), 0.05),
        "w_q": nrm(ks[21], (L, D_MODEL, PEER_HEADS * D_KEY), D_MODEL ** -0.5),
        "k1": nrm(ks[22], (L, N_KEYS, D_KEY // 2), (D_KEY // 2) ** -0.5),
        "k2": nrm(ks[23], (L, N_KEYS, D_KEY // 2), (D_KEY // 2) ** -0.5),
        "u_tab": nrm(ks[24], (L, N_EXPERTS, D_MODEL), D_MODEL ** -0.5),
        "v_tab": nrm(ks[25], (L, N_EXPERTS, D_MODEL), (PEER_HEADS * TOPK) ** -0.5),
        "g_final": 1.0 + nrm(ks[26], (D_MODEL,), 0.05),
    }


def reference(x_prompt, x_sample, state_conv, c_prompt, c_sample, w_ada, b_ada, g_mix, w_in,
              w_dw, b_dw, g_cn, b_cn, w_conv_out, g_v, b_v, w_s, b_s, w_gmlp_out, w_out,
              g_ffn, w_q, k1, k2, u_tab, v_tab, g_final):
    hp, hs = x_prompt, x_sample
    hist_p, hist_s, v_s = [], [], []
    for l in range(DEPTH):
        lp = (w_ada[l], b_ada[l], g_mix[l], w_in[l], w_dw[l], b_dw[l], g_cn[l], b_cn[l],
              w_conv_out[l], g_v[l], b_v[l], w_s[l], b_s[l], w_gmlp_out[l], w_out[l],
              g_ffn[l], w_q[l], k1[l], k2[l], u_tab[l], v_tab[l])
        zero_hist = jnp.zeros((x_prompt.shape[0], CONV_W - 1, D_CONV), x_prompt.dtype)
        hp, hp_hist, _ = layer(hp, c_prompt, zero_hist, *lp)
        hs, hs_hist, hs_v = layer(hs, c_sample, state_conv[l], *lp)
        hist_p.append(hp_hist)
        hist_s.append(hs_hist)
        v_s.append(hs_v)
    y_prompt = rmsnorm(hp, g_final)
    y_sample = rmsnorm(hs, g_final)
    return (y_prompt, y_sample, jnp.stack(hist_p), jnp.stack(hist_s), jnp.stack(v_s))
```

```python
import functools

import jax
import jax.numpy as jnp
from jax import lax
from jax.experimental import pallas as pl
from jax.experimental.pallas import tpu as pltpu

F32 = jnp.float32
BF16 = jnp.bfloat16

CONV_W = 31
GMLP_HEADS = 8
CHUNK = 128
PEER_HEADS = 8
N_KEYS = 128
TOPK = 16
EPS = 1e-6
NEG_INF = float("-inf")

VMEM_LIMIT_BYTES = 56 * 1024 * 1024

_NT = (((1,), (1,)), ((), ()))


def _params(sem):
    return pltpu.CompilerParams(dimension_semantics=sem, vmem_limit_bytes=VMEM_LIMIT_BYTES)


def _sigmoid(x):
    return 1.0 / (1.0 + jnp.exp(-x))


def _rms(x, g):
    return x * lax.rsqrt(jnp.mean(x * x, axis=-1, keepdims=True) + EPS) * g


def _ln(x, g, b):
    mu = jnp.mean(x, axis=-1, keepdims=True)
    xc = x - mu
    var = jnp.mean(xc * xc, axis=-1, keepdims=True)
    return xc * lax.rsqrt(var + EPS) * g + b


def _bdot(a, b):
    return jnp.dot(a.astype(BF16), b, preferred_element_type=F32)


def _ada_kernel(c_ref, w_ref, b_ref, o_ref):
    c = c_ref[...]
    s = c * _sigmoid(c)
    o_ref[...] = jnp.dot(s, w_ref[...], precision=lax.Precision.HIGHEST,
                         preferred_element_type=F32) + b_ref[...]


def _ada(c, w_ada, b_ada):
    n, d = c.shape
    w6 = w_ada.shape[1]
    bn = d
    return pl.pallas_call(
        _ada_kernel,
        grid=(w6 // bn,),
        in_specs=[pl.BlockSpec((n, d), lambda j: (0, 0)),
                  pl.BlockSpec((d, bn), lambda j: (0, j)),
                  pl.BlockSpec((1, bn), lambda j: (0, j))],
        out_specs=pl.BlockSpec((n, bn), lambda j: (0, j)),
        out_shape=jax.ShapeDtypeStruct((n, w6), F32),
        compiler_params=_params(("arbitrary",)),
        name="ada",
    )(c, w_ada, b_ada.reshape(1, w6))


def _mix_tail(x, mod, z, a_out, vn, sg, wgo_ref, wout_ref, gffn_ref, x1_ref, n2_ref):
    d = x.shape[-1]
    dc = d // 2
    u = z[:, 2 * dc:3 * dc]
    g_a = z[:, 4 * dc:4 * dc + d]
    g_b = z[:, 4 * dc + d:]
    b_br = _bdot(u * sg, wgo_ref[...])
    m = _sigmoid(g_a) * a_out + _sigmoid(g_b) * b_br
    gt1 = mod[:, 2 * d:3 * d]
    sh2 = mod[:, 3 * d:4 * d]
    sc2 = mod[:, 4 * d:5 * d]
    x1 = x + gt1 * _bdot(m, wout_ref[...])
    x1_ref[...] = x1.reshape(x1_ref.shape)
    n2 = _rms(x1, gffn_ref[...]) * (1.0 + sc2) + sh2
    n2_ref[...] = n2.astype(BF16).reshape(n2_ref.shape)


def _mix_kernel(x_ref, mod_ref, gmix_ref, win_ref, wdw_ref, bdw_ref, gcn_ref, bcn_ref, wco_ref,
                gv_ref, bv_ref, ws_ref, bsx_ref, wgo_ref, wout_ref, gffn_ref,
                x1_ref, n2_ref, hist_ref, aext_ref):
    j = pl.program_id(1)
    ts = x_ref.shape[1]
    d = x_ref.shape[2]
    dc = d // 2
    x = x_ref[0]
    mod = mod_ref[0]
    sh1 = mod[:, 0:d]
    sc1 = mod[:, d:2 * d]
    n = _rms(x, gmix_ref[...]) * (1.0 + sc1) + sh1
    z = _bdot(n, win_ref[...])
    a = z[:, 0:dc] * _sigmoid(z[:, dc:2 * dc])

    @pl.when(j == 0)
    def _():
        aext_ref[0:32, :] = jnp.zeros((32, dc), F32)

    @pl.when(j > 0)
    def _():
        aext_ref[0:32, :] = aext_ref[ts:ts + 32, :]

    aext_ref[32:32 + ts, :] = a
    hist_ref[0, 0] = aext_ref[ts + 2:ts + 32, :]

    acc = jnp.broadcast_to(bdw_ref[...], (ts, dc))
    for k in range(CONV_W):
        acc = acc + wdw_ref[k:k + 1, :] * aext_ref[2 + k:2 + k + ts, :]
    ac = _ln(acc, gcn_ref[...], bcn_ref[...])
    a_out = _bdot(ac * _sigmoid(ac), wco_ref[...])

    vn = _ln(z[:, 3 * dc:4 * dc], gv_ref[...], bv_ref[...])
    row = lax.broadcasted_iota(jnp.int32, (CHUNK, CHUNK), 0)
    col = lax.broadcasted_iota(jnp.int32, (CHUNK, CHUNK), 1)
    causal = col <= row
    first_head = col < (CHUNK // 2)
    wm = [jnp.where(causal, ws_ref[h], 0.0).astype(BF16) for h in range(GMLP_HEADS)]
    vb = vn.astype(BF16)
    chunks = []
    for c in range(ts // CHUNK):
        groups = []
        for p in range(GMLP_HEADS // 2):
            cols = vb[c * CHUNK:(c + 1) * CHUNK, p * CHUNK:(p + 1) * CHUNK]
            r0 = jnp.dot(wm[2 * p], cols, preferred_element_type=F32)
            r1 = jnp.dot(wm[2 * p + 1], cols, preferred_element_type=F32)
            groups.append(jnp.where(first_head, r0, r1))
        chunks.append(jnp.concatenate(groups, axis=1) + bsx_ref[...])
    sg = jnp.concatenate(chunks, axis=0)
    _mix_tail(x, mod, z, a_out, vn, sg, wgo_ref, wout_ref, gffn_ref, x1_ref, n2_ref)


def _mix1_kernel(x_ref, mod_ref, gmix_ref, win_ref, wdw_ref, bdw_ref, gcn_ref, bcn_ref, wco_ref,
                 gv_ref, bv_ref, ws0_ref, bs0_ref, wgo_ref, wout_ref, gffn_ref, st_ref,
                 x1_ref, n2_ref, a_ref, vn_ref):
    d = x_ref.shape[1]
    dc = d // 2
    x = x_ref[...]
    mod = mod_ref[...]
    sh1 = mod[:, 0:d]
    sc1 = mod[:, d:2 * d]
    n = _rms(x, gmix_ref[...]) * (1.0 + sc1) + sh1
    z = _bdot(n, win_ref[...])
    a = z[:, 0:dc] * _sigmoid(z[:, dc:2 * dc])
    a_ref[...] = a
    acc = bdw_ref[...] + wdw_ref[CONV_W - 1:CONV_W, :] * a
    for k in range(CONV_W - 1):
        acc = acc + wdw_ref[k:k + 1, :] * st_ref[k]
    ac = _ln(acc, gcn_ref[...], bcn_ref[...])
    a_out = _bdot(ac * _sigmoid(ac), wco_ref[...])
    vn = _ln(z[:, 3 * dc:4 * dc], gv_ref[...], bv_ref[...])
    vn_ref[...] = vn
    sg = ws0_ref[...] * vn + bs0_ref[...]
    _mix_tail(x, mod, z, a_out, vn, sg, wgo_ref, wout_ref, gffn_ref, x1_ref, n2_ref)


def _const_spec(shape):
    nd = len(shape)
    return pl.BlockSpec(shape, lambda *_: (0,) * nd)


def _mix_weights(p):
    d = p["w_in"].shape[0]
    dc = d // 2
    row = lambda v: v.reshape(1, -1)
    return [row(p["g_mix"]), p["w_in"].astype(BF16), p["w_dw"], row(p["b_dw"]), row(p["g_cn"]),
            row(p["b_cn"]), p["w_conv_out"].astype(BF16), row(p["g_v"]), row(p["b_v"])], \
           [p["w_gmlp_out"].astype(BF16), p["w_out"].astype(BF16), row(p["g_ffn"])]


def _mix_prompt(x, mod, p, ts):
    nb, s, d = x.shape
    dc = d // 2
    pre, post = _mix_weights(p)
    hd = dc // GMLP_HEADS
    bsx = jnp.repeat(p["b_s"].T, hd, axis=1)
    consts = pre + [p["w_s"], bsx] + post
    mod3 = mod.reshape(nb, 1, mod.shape[-1])
    return pl.pallas_call(
        _mix_kernel,
        grid=(nb, s // ts),
        in_specs=[pl.BlockSpec((1, ts, d), lambda b, j: (b, j, 0)),
                  pl.BlockSpec((1, 1, mod.shape[-1]), lambda b, j: (b, 0, 0))]
                 + [_const_spec(c.shape) for c in consts],
        out_specs=[pl.BlockSpec((1, ts, d), lambda b, j: (b, j, 0)),
                   pl.BlockSpec((1, ts, d), lambda b, j: (b, j, 0)),
                   pl.BlockSpec((1, 1, CONV_W - 1, dc), lambda b, j: (0, b, 0, 0))],
        out_shape=[jax.ShapeDtypeStruct((nb, s, d), F32),
                   jax.ShapeDtypeStruct((nb, s, d), BF16),
                   jax.ShapeDtypeStruct((1, nb, CONV_W - 1, dc), F32)],
        scratch_shapes=[pltpu.VMEM((ts + 32, dc), F32)],
        compiler_params=_params(("arbitrary", "arbitrary")),
        name="mix",
    )(x, mod3, *consts)


def _mix_sample(x, mod, state, p):
    n, d = x.shape
    dc = d // 2
    pre, post = _mix_weights(p)
    hd = dc // GMLP_HEADS
    ws0 = jnp.repeat(p["w_s"][:, 0, 0], hd).reshape(1, dc)
    bs0 = jnp.repeat(p["b_s"][:, 0], hd).reshape(1, dc)
    st = jnp.transpose(state, (1, 0, 2))
    args = [x, mod] + pre + [ws0, bs0] + post + [st]
    return pl.pallas_call(
        _mix1_kernel,
        grid=(1,),
        in_specs=[_const_spec(a.shape) for a in args],
        out_specs=[_const_spec((n, d)), _const_spec((n, d)), _const_spec((n, dc)), _const_spec((n, dc))],
        out_shape=[jax.ShapeDtypeStruct((n, d), F32), jax.ShapeDtypeStruct((n, d), BF16),
                   jax.ShapeDtypeStruct((n, dc), F32), jax.ShapeDtypeStruct((n, dc), F32)],
        compiler_params=_params(("arbitrary",)),
        name="mix1",
    )(*args)


def _top16(s):
    nk, tr = s.shape
    iota = lax.broadcasted_iota(jnp.int32, (nk, tr), 0)
    rank = jnp.full((nk, tr), float(TOPK), F32)
    vals = []
    for r in range(TOPK):
        m = jnp.max(s, axis=0, keepdims=True)
        idx = jnp.min(jnp.where(s == m, iota, nk), axis=0, keepdims=True)
        hit = iota == idx
        rank = jnp.where(hit, float(r), rank)
        s = jnp.where(hit, NEG_INF, s)
        vals.append(m)
    return jnp.concatenate(vals, axis=0), rank


def _route_kernel(n2_ref, wqt_ref, k1_ref, k2_ref, nsel_ref, e1_ref, rb_ref, e2_ref):
    tr = n2_ref.shape[0]
    n2 = n2_ref[...]
    k1 = k1_ref[...]
    k2 = k2_ref[...]
    half = k1.shape[1]

    i8 = lax.broadcasted_iota(jnp.int32, (8, tr), 0)
    i16 = lax.broadcasted_iota(jnp.int32, (16, tr), 0)
    ids = jnp.concatenate([i16] + [r * TOPK + i8 for r in range(1, 8)] + [(i8 + 8) * TOPK], axis=0)
    ok = jnp.concatenate([i16 >= 0] + [(r + 1) * (i8 + 1) <= TOPK for r in range(1, 8)] + [i8 >= 0], axis=0)
    big = TOPK * TOPK

    def head(h, carry):
        off = pl.multiple_of(h * 2 * half, 2 * half)
        qt = lax.dot_general(wqt_ref[pl.ds(off, 2 * half), :], n2, _NT, preferred_element_type=F32)
        s1 = jnp.dot(k1, qt[0:half].astype(BF16), preferred_element_type=F32)
        s2 = jnp.dot(k2, qt[half:].astype(BF16), preferred_element_type=F32)
        sc1, rank1 = _top16(s1)
        sc2, rank2 = _top16(s2)
        cand0 = jnp.concatenate([sc1[0:1] + sc2] + [sc1[r:r + 1] + sc2[0:8] for r in range(1, 8)]
                                + [sc1[8:16] + sc2[0:1]], axis=0)
        cand0 = jnp.where(ok, cand0, NEG_INF)
        cand = cand0
        sel = jnp.zeros(cand.shape, jnp.bool_)
        for _ in range(TOPK):
            m = jnp.max(cand, axis=0, keepdims=True)
            idx = jnp.min(jnp.where(cand == m, ids, big), axis=0, keepdims=True)
            hit = ids == idx
            sel = jnp.logical_or(sel, hit)
            cand = jnp.where(hit, NEG_INF, cand)
        top = sc1[0:1] + sc2[0:1]
        z = jnp.sum(jnp.where(sel, jnp.exp(cand0 - top), 0.0), axis=0, keepdims=True)
        self32 = jnp.where(sel, 1.0, 0.0)
        counts = [jnp.sum(self32[0:16], axis=0, keepdims=True)]
        counts += [jnp.sum(self32[8 + 8 * r:16 + 8 * r], axis=0, keepdims=True) for r in range(1, 8)]
        tail = self32[72:80]
        nsel = jnp.zeros((N_KEYS, tr), F32)
        for r in range(TOPK):
            n_r = counts[r] if r < 8 else tail[r - 8:r - 7]
            nsel = jnp.where(rank1 == float(r), n_r, nsel)
        nsel_ref[h] = nsel
        e1_ref[h] = jnp.exp(s1 - sc1[0:1]) / z
        rb_ref[h] = rank2
        e2_ref[h] = jnp.exp(s2 - sc2[0:1])
        return carry

    lax.fori_loop(0, PEER_HEADS, head, 0)


def _route(n2, wqt, k1, k2, tr):
    t, d = n2.shape
    outs = jax.ShapeDtypeStruct((PEER_HEADS, N_KEYS, t), F32)
    ospec = pl.BlockSpec((PEER_HEADS, N_KEYS, tr), lambda i: (0, 0, i))
    return pl.pallas_call(
        _route_kernel,
        grid=(t // tr,),
        in_specs=[pl.BlockSpec((tr, d), lambda i: (i, 0)),
                  _const_spec(wqt.shape), _const_spec(k1.shape), _const_spec(k2.shape)],
        out_specs=[ospec] * 4,
        out_shape=[outs] * 4,
        compiler_params=_params(("arbitrary",)),
        name="route",
    )(n2, wqt, k1, k2)


def _gelu(x):
    return 0.5 * x * (1.0 + lax.erf(x * (2.0 ** -0.5)))


def _experts_kernel(n2_ref, u_ref, vt_ref, nsel_ref, e1_ref, rb_ref, e2_ref, x1_ref, gt2_ref, gfin_ref,
                    y_ref, h_scr, a_scr, acc_scr):
    c = pl.program_id(1)
    ec = u_ref.shape[0]
    t = n2_ref.shape[0]
    keys_per_chunk = ec // N_KEYS
    lane_tiles = t // 128

    @pl.when(c == 0)
    def _():
        acc_scr[...] = jnp.zeros(acc_scr.shape, F32)

    h_scr[...] = lax.dot_general(u_ref[...], n2_ref[...], _NT, preferred_element_type=F32)

    for j in range(keys_per_chunk):
        rows = slice(j * N_KEYS, (j + 1) * N_KEYS)

        def tile(lt, carry, j=j, rows=rows):
            lanes = pl.ds(pl.multiple_of(lt * 128, 128), 128)
            w = jnp.zeros((N_KEYS, 128), F32)
            for h in range(PEER_HEADS):
                n_row = nsel_ref[h, j:j + 1, lanes]
                e_row = e1_ref[h, j:j + 1, lanes]
                w = w + jnp.where(rb_ref[h, :, lanes] < n_row, e2_ref[h, :, lanes], 0.0) * e_row
            a_scr[rows, lanes] = (_gelu(h_scr[rows, lanes]) * w).astype(BF16)
            return carry

        lax.fori_loop(0, lane_tiles, tile, 0)

    acc_scr[...] += jnp.dot(vt_ref[...], a_scr[...], preferred_element_type=F32)

    @pl.when(c == pl.num_programs(1) - 1)
    def _():
        x2 = x1_ref[...] + gt2_ref[...] * acc_scr[...].T
        y_ref[...] = _rms(x2, gfin_ref[...])


def _experts(n2, u, vt, route, x1, gt2, gt2_spec, gfin, t, ec):
    tot, d = n2.shape
    ne = u.shape[0]
    rspec = pl.BlockSpec((PEER_HEADS, N_KEYS, t), lambda i, c: (0, 0, i))
    cspec = pl.BlockSpec((PEER_HEADS, ec // N_KEYS, t), lambda i, c: (0, c, i))
    assert (ec // N_KEYS) % 8 == 0
    return pl.pallas_call(
        _experts_kernel,
        grid=(tot // t, ne // ec),
        in_specs=[pl.BlockSpec((t, d), lambda i, c: (i, 0)),
                  pl.BlockSpec((ec, d), lambda i, c: (c, 0)),
                  pl.BlockSpec((d, ec), lambda i, c: (0, c)),
                  cspec, cspec, rspec, rspec,
                  pl.BlockSpec((t, d), lambda i, c: (i, 0)),
                  gt2_spec,
                  pl.BlockSpec((1, d), lambda i, c: (0, 0))],
        out_specs=pl.BlockSpec((t, d), lambda i, c: (i, 0)),
        out_shape=jax.ShapeDtypeStruct((tot, d), F32),
        scratch_shapes=[pltpu.VMEM((ec, t), F32), pltpu.VMEM((ec, t), BF16), pltpu.VMEM((d, t), F32)],
        compiler_params=_params(("arbitrary", "arbitrary")),
        name="experts",
    )(n2, u, vt, *route, x1, gt2, gfin)


def _pick(n, cap):
    best = 128
    for b in range(128, cap + 1, 128):
        if n % b == 0:
            best = b
    return best


def kernel(x_prompt, x_sample, state_conv, c_prompt, c_sample, w_ada, b_ada, g_mix, w_in, w_dw, b_dw,
           g_cn, b_cn, w_conv_out, g_v, b_v, w_s, b_s, w_gmlp_out, w_out, g_ffn, w_q, k1, k2,
           u_tab, v_tab, g_final):
    depth = w_ada.shape[0]
    assert depth == 1 and x_sample.shape[1] == 1
    nb, s, d = x_prompt.shape
    ns = x_sample.shape[0]
    assert s % CHUNK == 0 and ns % 128 == 0 and d % 256 == 0
    l = 0
    p = dict(g_mix=g_mix[l], w_in=w_in[l], w_dw=w_dw[l], b_dw=b_dw[l], g_cn=g_cn[l], b_cn=b_cn[l],
             w_conv_out=w_conv_out[l], g_v=g_v[l], b_v=b_v[l], w_s=w_s[l], b_s=b_s[l],
             w_gmlp_out=w_gmlp_out[l], w_out=w_out[l], g_ffn=g_ffn[l])

    mod = _ada(jnp.concatenate([c_prompt, c_sample], axis=0), w_ada[l], b_ada[l])
    mod_p, mod_s = mod[:nb], mod[nb:]

    x1_p, n2_p, hist_p = _mix_prompt(x_prompt, mod_p, p, _pick(s, 512))
    x1_s, n2_s, a_s, vn_s = _mix_sample(x_sample.reshape(ns, d), mod_s, state_conv[l], p)

    wqt = w_q[l].T.astype(BF16)
    k1b = k1[l].astype(BF16)
    k2b = k2[l].astype(BF16)
    u = u_tab[l].astype(BF16)
    vt = v_tab[l].T.astype(BF16)
    gfin = g_final.reshape(1, d)
    ec = _pick(u.shape[0], 1024)

    n2_pf = n2_p.reshape(nb * s, d)
    route_p = _route(n2_pf, wqt, k1b, k2b, _pick(nb * s, 256))
    tp = _pick(s, 512)
    per_seq = s // tp
    gt2_p = mod_p[:, 5 * d:].reshape(nb, 1, d)
    y_p = _experts(n2_pf, u, vt, route_p, x1_p.reshape(nb * s, d), gt2_p,
                   pl.BlockSpec((None, 1, d), lambda i, c: (i // per_seq, 0, 0)), gfin, tp, ec)

    route_s = _route(n2_s, wqt, k1b, k2b, _pick(ns, 256))
    ts_ = _pick(ns, 512)
    y_s = _experts(n2_s, u, vt, route_s, x1_s, mod_s[:, 5 * d:],
                   pl.BlockSpec((ts_, d), lambda i, c: (i, 0)), gfin, ts_, ec)

    hist_s = jnp.concatenate([state_conv[l][:, 1:], a_s[:, None, :]], axis=1)[None]
    return (y_p.reshape(nb, s, d), y_s.reshape(ns, 1, d), hist_p, hist_s, vn_s.reshape(1, ns, 1, d // 2))
```

```python
import functools

import jax
import jax.numpy as jnp
from jax import lax
from jax.experimental import pallas as pl
from jax.experimental.pallas import tpu as pltpu

F32 = jnp.float32
BF16 = jnp.bfloat16

CONV_W = 31
GMLP_HEADS = 8
CHUNK = 128
PEER_HEADS = 8
N_KEYS = 128
TOPK = 16
EPS = 1e-6
NEG_INF = float("-inf")

VMEM_LIMIT_BYTES = 56 * 1024 * 1024

_NT = (((1,), (1,)), ((), ()))


def _params(sem):
    return pltpu.CompilerParams(dimension_semantics=sem, vmem_limit_bytes=VMEM_LIMIT_BYTES)


def _sigmoid(x):
    return 1.0 / (1.0 + jnp.exp(-x))


def _rms(x, g):
    return x * lax.rsqrt(jnp.mean(x * x, axis=-1, keepdims=True) + EPS) * g


def _ln(x, g, b):
    mu = jnp.mean(x, axis=-1, keepdims=True)
    xc = x - mu
    var = jnp.mean(xc * xc, axis=-1, keepdims=True)
    return xc * lax.rsqrt(var + EPS) * g + b


def _bdot(a, b):
    return jnp.dot(a.astype(BF16), b, preferred_element_type=F32)


def _ada_kernel(c_ref, w_ref, b_ref, o_ref):
    c = c_ref[...]
    s = c * _sigmoid(c)
    o_ref[...] = jnp.dot(s, w_ref[...], precision=lax.Precision.HIGHEST,
                         preferred_element_type=F32) + b_ref[...]


def _ada(c, w_ada, b_ada):
    n, d = c.shape
    w6 = w_ada.shape[1]
    bn = d
    return pl.pallas_call(
        _ada_kernel,
        grid=(w6 // bn,),
        in_specs=[pl.BlockSpec((n, d), lambda j: (0, 0)),
                  pl.BlockSpec((d, bn), lambda j: (0, j)),
                  pl.BlockSpec((1, bn), lambda j: (0, j))],
        out_specs=pl.BlockSpec((n, bn), lambda j: (0, j)),
        out_shape=jax.ShapeDtypeStruct((n, w6), F32),
        compiler_params=_params(("arbitrary",)),
        name="ada",
    )(c, w_ada, b_ada.reshape(1, w6))


def _mix_tail(x, mod, z, a_out, sg, wgo_ref, wout_ref, gffn_ref, x1_ref, n2t_ref):
    d = x.shape[-1]
    dc = d // 2
    u = z[:, 2 * dc:3 * dc]
    g_a = z[:, 4 * dc:4 * dc + d]
    g_b = z[:, 4 * dc + d:]
    b_br = _bdot(u * sg, wgo_ref[...])
    m = _sigmoid(g_a) * a_out + _sigmoid(g_b) * b_br
    gt1 = mod[:, 2 * d:3 * d]
    sh2 = mod[:, 3 * d:4 * d]
    sc2 = mod[:, 4 * d:5 * d]
    x1 = x + gt1 * _bdot(m, wout_ref[...])
    x1_ref[...] = x1.reshape(x1_ref.shape)
    n2 = _rms(x1, gffn_ref[...]) * (1.0 + sc2) + sh2
    n2t_ref[...] = n2.T.astype(BF16)


def _mix_kernel(x_ref, mod_ref, gmix_ref, win_ref, wdw_ref, bdw_ref, gcn_ref, bcn_ref, wco_ref,
                gv_ref, bv_ref, ws_ref, bsx_ref, wgo_ref, wout_ref, gffn_ref,
                x1_ref, n2t_ref, hist_ref, aext_ref):
    j = pl.program_id(1)
    ts = x_ref.shape[1]
    d = x_ref.shape[2]
    dc = d // 2
    x = x_ref[0]
    mod = mod_ref[0]
    sh1 = mod[:, 0:d]
    sc1 = mod[:, d:2 * d]
    n = _rms(x, gmix_ref[...]) * (1.0 + sc1) + sh1
    z = _bdot(n, win_ref[...])
    a = z[:, 0:dc] * _sigmoid(z[:, dc:2 * dc])

    @pl.when(j == 0)
    def _():
        aext_ref[0:32, :] = jnp.zeros((32, dc), F32)

    @pl.when(j > 0)
    def _():
        aext_ref[0:32, :] = aext_ref[ts:ts + 32, :]

    aext_ref[32:32 + ts, :] = a
    hist_ref[0, 0] = aext_ref[ts + 2:ts + 32, :]

    acc = jnp.broadcast_to(bdw_ref[...], (ts, dc))
    for k in range(CONV_W):
        acc = acc + wdw_ref[k:k + 1, :] * aext_ref[2 + k:2 + k + ts, :]
    ac = _ln(acc, gcn_ref[...], bcn_ref[...])
    a_out = _bdot(ac * _sigmoid(ac), wco_ref[...])

    vn = _ln(z[:, 3 * dc:4 * dc], gv_ref[...], bv_ref[...])
    row = lax.broadcasted_iota(jnp.int32, (CHUNK, CHUNK), 0)
    col = lax.broadcasted_iota(jnp.int32, (CHUNK, CHUNK), 1)
    causal = col <= row
    first_head = col < (CHUNK // 2)
    wm = [jnp.where(causal, ws_ref[h], 0.0).astype(BF16) for h in range(GMLP_HEADS)]
    vb = vn.astype(BF16)
    chunks = []
    for c in range(ts // CHUNK):
        groups = []
        for p in range(GMLP_HEADS // 2):
            cols = vb[c * CHUNK:(c + 1) * CHUNK, p * CHUNK:(p + 1) * CHUNK]
            r0 = jnp.dot(wm[2 * p], cols, preferred_element_type=F32)
            r1 = jnp.dot(wm[2 * p + 1], cols, preferred_element_type=F32)
            groups.append(jnp.where(first_head, r0, r1))
        chunks.append(jnp.concatenate(groups, axis=1) + bsx_ref[...])
    sg = jnp.concatenate(chunks, axis=0)
    _mix_tail(x, mod, z, a_out, sg, wgo_ref, wout_ref, gffn_ref, x1_ref, n2t_ref)


def _mix1_kernel(x_ref, mod_ref, gmix_ref, win_ref, wdw_ref, bdw_ref, gcn_ref, bcn_ref, wco_ref,
                 gv_ref, bv_ref, ws0_ref, bs0_ref, wgo_ref, wout_ref, gffn_ref, st_ref,
                 x1_ref, n2t_ref, a_ref, vn_ref):
    d = x_ref.shape[1]
    dc = d // 2
    x = x_ref[...]
    mod = mod_ref[...]
    sh1 = mod[:, 0:d]
    sc1 = mod[:, d:2 * d]
    n = _rms(x, gmix_ref[...]) * (1.0 + sc1) + sh1
    z = _bdot(n, win_ref[...])
    a = z[:, 0:dc] * _sigmoid(z[:, dc:2 * dc])
    a_ref[...] = a
    acc = bdw_ref[...] + wdw_ref[CONV_W - 1:CONV_W, :] * a
    for k in range(CONV_W - 1):
        acc = acc + wdw_ref[k:k + 1, :] * st_ref[k]
    ac = _ln(acc, gcn_ref[...], bcn_ref[...])
    a_out = _bdot(ac * _sigmoid(ac), wco_ref[...])
    vn = _ln(z[:, 3 * dc:4 * dc], gv_ref[...], bv_ref[...])
    vn_ref[...] = vn
    sg = ws0_ref[...] * vn + bs0_ref[...]
    _mix_tail(x, mod, z, a_out, sg, wgo_ref, wout_ref, gffn_ref, x1_ref, n2t_ref)


def _const_spec(shape):
    nd = len(shape)
    return pl.BlockSpec(shape, lambda *_: (0,) * nd)


def _mix_weights(p):
    d = p["w_in"].shape[0]
    dc = d // 2
    row = lambda v: v.reshape(1, -1)
    return [row(p["g_mix"]), p["w_in"].astype(BF16), p["w_dw"], row(p["b_dw"]), row(p["g_cn"]),
            row(p["b_cn"]), p["w_conv_out"].astype(BF16), row(p["g_v"]), row(p["b_v"])], \
           [p["w_gmlp_out"].astype(BF16), p["w_out"].astype(BF16), row(p["g_ffn"])]


def _mix_prompt(x, mod, p, ts):
    nb, s, d = x.shape
    dc = d // 2
    pre, post = _mix_weights(p)
    hd = dc // GMLP_HEADS
    bsx = jnp.repeat(p["b_s"].T, hd, axis=1)
    consts = pre + [p["w_s"], bsx] + post
    mod3 = mod.reshape(nb, 1, mod.shape[-1])
    return pl.pallas_call(
        _mix_kernel,
        grid=(nb, s // ts),
        in_specs=[pl.BlockSpec((1, ts, d), lambda b, j: (b, j, 0)),
                  pl.BlockSpec((1, 1, mod.shape[-1]), lambda b, j: (b, 0, 0))]
                 + [_const_spec(c.shape) for c in consts],
        out_specs=[pl.BlockSpec((1, ts, d), lambda b, j: (b, j, 0)),
                   pl.BlockSpec((d, ts), lambda b, j: (0, b * (s // ts) + j)),
                   pl.BlockSpec((1, 1, CONV_W - 1, dc), lambda b, j: (0, b, 0, 0))],
        out_shape=[jax.ShapeDtypeStruct((nb, s, d), F32),
                   jax.ShapeDtypeStruct((d, nb * s), BF16),
                   jax.ShapeDtypeStruct((1, nb, CONV_W - 1, dc), F32)],
        scratch_shapes=[pltpu.VMEM((ts + 32, dc), F32)],
        compiler_params=_params(("arbitrary", "arbitrary")),
        name="mix",
    )(x, mod3, *consts)


def _mix_sample(x, mod, state, p):
    n, d = x.shape
    dc = d // 2
    pre, post = _mix_weights(p)
    hd = dc // GMLP_HEADS
    ws0 = jnp.repeat(p["w_s"][:, 0, 0], hd).reshape(1, dc)
    bs0 = jnp.repeat(p["b_s"][:, 0], hd).reshape(1, dc)
    st = jnp.transpose(state, (1, 0, 2))
    args = [x, mod] + pre + [ws0, bs0] + post + [st]
    return pl.pallas_call(
        _mix1_kernel,
        grid=(1,),
        in_specs=[_const_spec(a.shape) for a in args],
        out_specs=[_const_spec((n, d)), _const_spec((d, n)), _const_spec((n, dc)), _const_spec((n, dc))],
        out_shape=[jax.ShapeDtypeStruct((n, d), F32), jax.ShapeDtypeStruct((d, n), BF16),
                   jax.ShapeDtypeStruct((n, dc), F32), jax.ShapeDtypeStruct((n, dc), F32)],
        compiler_params=_params(("arbitrary",)),
        name="mix1",
    )(*args)


def _top16(s):
    nk, tr = s.shape
    iota = lax.broadcasted_iota(jnp.int32, (nk, tr), 0)
    rank = jnp.full((nk, tr), float(TOPK), F32)
    vals = []
    for r in range(TOPK):
        m = jnp.max(s, axis=0, keepdims=True)
        idx = jnp.min(jnp.where(s == m, iota, nk), axis=0, keepdims=True)
        hit = iota == idx
        rank = jnp.where(hit, float(r), rank)
        s = jnp.where(hit, NEG_INF, s)
        vals.append(m)
    return jnp.concatenate(vals, axis=0), rank


def _route_kernel(n2t_ref, wqt_ref, k1_ref, k2_ref, nsel_ref, e1_ref, rb_ref, e2_ref):
    tr = n2t_ref.shape[1]
    n2t = n2t_ref[...]
    k1 = k1_ref[...]
    k2 = k2_ref[...]
    half = k1.shape[1]

    i8 = lax.broadcasted_iota(jnp.int32, (8, tr), 0)
    i16 = lax.broadcasted_iota(jnp.int32, (16, tr), 0)
    ids = jnp.concatenate([i16] + [r * TOPK + i8 for r in range(1, 8)] + [(i8 + 8) * TOPK], axis=0)
    ok = jnp.concatenate([i16 >= 0] + [(r + 1) * (i8 + 1) <= TOPK for r in range(1, 8)] + [i8 >= 0], axis=0)
    big = TOPK * TOPK

    def head(h, carry):
        off = pl.multiple_of(h * 2 * half, 2 * half)
        qt = jnp.dot(wqt_ref[pl.ds(off, 2 * half), :], n2t, preferred_element_type=F32)
        s1 = jnp.dot(k1, qt[0:half].astype(BF16), preferred_element_type=F32)
        s2 = jnp.dot(k2, qt[half:].astype(BF16), preferred_element_type=F32)
        sc1, rank1 = _top16(s1)
        sc2, rank2 = _top16(s2)
        cand0 = jnp.concatenate([sc1[0:1] + sc2] + [sc1[r:r + 1] + sc2[0:8] for r in range(1, 8)]
                                + [sc1[8:16] + sc2[0:1]], axis=0)
        cand0 = jnp.where(ok, cand0, NEG_INF)
        cand = cand0
        sel = jnp.zeros(cand.shape, jnp.bool_)
        for _ in range(TOPK):
            m = jnp.max(cand, axis=0, keepdims=True)
            idx = jnp.min(jnp.where(cand == m, ids, big), axis=0, keepdims=True)
            hit = ids == idx
            sel = jnp.logical_or(sel, hit)
            cand = jnp.where(hit, NEG_INF, cand)
        top = sc1[0:1] + sc2[0:1]
        z = jnp.sum(jnp.where(sel, jnp.exp(cand0 - top), 0.0), axis=0, keepdims=True)
        self32 = jnp.where(sel, 1.0, 0.0)
        counts = [jnp.sum(self32[0:16], axis=0, keepdims=True)]
        counts += [jnp.sum(self32[8 + 8 * r:16 + 8 * r], axis=0, keepdims=True) for r in range(1, 8)]
        tail = self32[72:80]
        nsel = jnp.zeros((N_KEYS, tr), F32)
        for r in range(TOPK):
            n_r = counts[r] if r < 8 else tail[r - 8:r - 7]
            nsel = jnp.where(rank1 == float(r), n_r, nsel)
        nsel_ref[h] = nsel
        e1_ref[h] = jnp.exp(s1 - sc1[0:1]) / z
        rb_ref[h] = rank2
        e2_ref[h] = jnp.exp(s2 - sc2[0:1])
        return carry

    lax.fori_loop(0, PEER_HEADS, head, 0)


def _route(n2t, wqt, k1, k2, tr):
    d, t = n2t.shape
    ospec = pl.BlockSpec((PEER_HEADS, N_KEYS, tr), lambda i: (0, 0, i))
    return pl.pallas_call(
        _route_kernel,
        grid=(t // tr,),
        in_specs=[pl.BlockSpec((d, tr), lambda i: (0, i)),
                  _const_spec(wqt.shape), _const_spec(k1.shape), _const_spec(k2.shape)],
        out_specs=[ospec] * 4,
        out_shape=[jax.ShapeDtypeStruct((PEER_HEADS, N_KEYS, t), F32)] * 4,
        compiler_params=_params(("arbitrary",)),
        name="route",
    )(n2t, wqt, k1, k2)


def _gelu(x):
    return 0.5 * x * (1.0 + lax.erf(x * (2.0 ** -0.5)))


SUB = 256
SLAB = 16


def _experts_kernel(n2t_ref, u_ref, vt_ref, nsel_ref, e1_ref, rb32_ref, e232_ref, x1_ref, gt2_ref, gfin_ref,
                    y_ref, h_scr, a_scr, acc_scr, rb_ref, e2_ref, xt_scr):
    c = pl.program_id(1)
    ec = u_ref.shape[0]
    t = n2t_ref.shape[1]
    n_sub = ec // SUB
    keys_per_sub = SUB // N_KEYS
    lane_tiles = t // 128

    @pl.when(c == 0)
    def _():
        acc_scr[...] = jnp.zeros(acc_scr.shape, F32)
        xt_scr[...] = n2t_ref[...]
        for h in range(PEER_HEADS):
            rb_ref[h] = rb32_ref[h].astype(BF16)
            e2_ref[h] = e232_ref[h].astype(BF16)

    def hidden(s):
        h_scr[s % 2] = jnp.dot(u_ref[s * SUB:(s + 1) * SUB, :], xt_scr[...], preferred_element_type=F32)

    def gate(s):
        for jj in range(keys_per_sub):
            j = s * keys_per_sub + jj
            for lt in range(lane_tiles):
                lanes = slice(lt * 128, (lt + 1) * 128)
                w = [None] * (N_KEYS // SLAB)
                for h in range(PEER_HEADS):
                    n16 = jnp.broadcast_to(nsel_ref[h, j:j + 1, lanes], (SLAB, 128)).astype(BF16)
                    e16 = jnp.broadcast_to(e1_ref[h, j:j + 1, lanes], (SLAB, 128)).astype(BF16)
                    for k in range(N_KEYS // SLAB):
                        keys = slice(k * SLAB, (k + 1) * SLAB)
                        term = jnp.where(rb_ref[h, keys, lanes] < n16, e2_ref[h, keys, lanes], 0.0) * e16
                        w[k] = term if w[k] is None else w[k] + term
                for k in range(N_KEYS // SLAB):
                    rows = slice(jj * N_KEYS + k * SLAB, jj * N_KEYS + (k + 1) * SLAB)
                    act = _gelu(h_scr[s % 2, rows, lanes]).astype(BF16)
                    a_scr[s % 2, rows, lanes] = act * w[k]

    def combine(s):
        acc_scr[...] += jnp.dot(vt_ref[:, s * SUB:(s + 1) * SUB], a_scr[s % 2], preferred_element_type=F32)

    hidden(0)
    for s in range(n_sub):
        if s + 1 < n_sub:
            hidden(s + 1)
        gate(s)
        combine(s)

    @pl.when(c == pl.num_programs(1) - 1)
    def _():
        x2 = x1_ref[...] + gt2_ref[...] * acc_scr[...].T
        y_ref[...] = _rms(x2, gfin_ref[...])


def _experts(n2t, u, vt, route, x1, gt2, gt2_spec, gfin, t, ec):
    d, tot = n2t.shape
    ne = u.shape[0]
    assert ec % SUB == 0 and (ec // N_KEYS) % 8 == 0 and t % 128 == 0
    rspec = pl.BlockSpec((PEER_HEADS, N_KEYS, t), lambda i, c: (0, 0, i))
    cspec = pl.BlockSpec((PEER_HEADS, ec // N_KEYS, t), lambda i, c: (0, c, i))
    return pl.pallas_call(
        _experts_kernel,
        grid=(tot // t, ne // ec),
        in_specs=[pl.BlockSpec((d, t), lambda i, c: (0, i)),
                  pl.BlockSpec((ec, d), lambda i, c: (c, 0)),
                  pl.BlockSpec((d, ec), lambda i, c: (0, c)),
                  cspec, cspec, rspec, rspec,
                  pl.BlockSpec((t, d), lambda i, c: (i, 0)),
                  gt2_spec,
                  pl.BlockSpec((1, d), lambda i, c: (0, 0))],
        out_specs=pl.BlockSpec((t, d), lambda i, c: (i, 0)),
        out_shape=jax.ShapeDtypeStruct((tot, d), F32),
        scratch_shapes=[pltpu.VMEM((2, SUB, t), F32), pltpu.VMEM((2, SUB, t), BF16), pltpu.VMEM((d, t), F32),
                        pltpu.VMEM((PEER_HEADS, N_KEYS, t), BF16), pltpu.VMEM((PEER_HEADS, N_KEYS, t), BF16),
                        pltpu.VMEM((d, t), BF16)],
        compiler_params=_params(("arbitrary", "arbitrary")),
        name="experts",
    )(n2t, u, vt, *route, x1, gt2, gfin)


def _pick(n, cap):
    best = 128
    for b in range(128, cap + 1, 128):
        if n % b == 0:
            best = b
    return best


def kernel(x_prompt, x_sample, state_conv, c_prompt, c_sample, w_ada, b_ada, g_mix, w_in, w_dw, b_dw,
           g_cn, b_cn, w_conv_out, g_v, b_v, w_s, b_s, w_gmlp_out, w_out, g_ffn, w_q, k1, k2,
           u_tab, v_tab, g_final):
    depth = w_ada.shape[0]
    assert depth == 1 and x_sample.shape[1] == 1
    nb, s, d = x_prompt.shape
    ns = x_sample.shape[0]
    assert s % CHUNK == 0 and ns % 128 == 0 and d % 256 == 0
    l = 0
    p = dict(g_mix=g_mix[l], w_in=w_in[l], w_dw=w_dw[l], b_dw=b_dw[l], g_cn=g_cn[l], b_cn=b_cn[l],
             w_conv_out=w_conv_out[l], g_v=g_v[l], b_v=b_v[l], w_s=w_s[l], b_s=b_s[l],
             w_gmlp_out=w_gmlp_out[l], w_out=w_out[l], g_ffn=g_ffn[l])

    mod = _ada(jnp.concatenate([c_prompt, c_sample], axis=0), w_ada[l], b_ada[l])
    mod_p, mod_s = mod[:nb], mod[nb:]

    x1_p, n2t_p, hist_p = _mix_prompt(x_prompt, mod_p, p, _pick(s, 512))
    x1_s, n2t_s, a_s, vn_s = _mix_sample(x_sample.reshape(ns, d), mod_s, state_conv[l], p)

    wqt = w_q[l].T.astype(BF16)
    k1b = k1[l].astype(BF16)
    k2b = k2[l].astype(BF16)
    u = u_tab[l].astype(BF16)
    vt = v_tab[l].T.astype(BF16)
    gfin = g_final.reshape(1, d)
    ec = _pick(u.shape[0], 1024)

    route_p = _route(n2t_p, wqt, k1b, k2b, _pick(nb * s, 256))
    tp = _pick(s, 512)
    per_seq = s // tp
    gt2_p = mod_p[:, 5 * d:].reshape(nb, 1, d)
    y_p = _experts(n2t_p, u, vt, route_p, x1_p.reshape(nb * s, d), gt2_p,
                   pl.BlockSpec((None, 1, d), lambda i, c: (i // per_seq, 0, 0)), gfin, tp, ec)

    route_s = _route(n2t_s, wqt, k1b, k2b, _pick(ns, 256))
    ts_ = _pick(ns, 512)
    y_s = _experts(n2t_s, u, vt, route_s, x1_s, mod_s[:, 5 * d:],
                   pl.BlockSpec((ts_, d), lambda i, c: (i, 0)), gfin, ts_, ec)

    hist_s = jnp.concatenate([state_conv[l][:, 1:], a_s[:, None, :]], axis=1)[None]
    return (y_p.reshape(nb, s, d), y_s.reshape(ns, 1, d), hist_p, hist_s, vn_s.reshape(1, ns, 1, d // 2))
```

```python
import functools

import jax
import jax.numpy as jnp
from jax import lax
from jax.experimental import pallas as pl
from jax.experimental.pallas import tpu as pltpu

F32 = jnp.float32
BF16 = jnp.bfloat16

CONV_W = 31
GMLP_HEADS = 8
CHUNK = 128
PEER_HEADS = 8
N_KEYS = 128
TOPK = 16
EPS = 1e-6
NEG_INF = float("-inf")

VMEM_LIMIT_BYTES = 56 * 1024 * 1024

_NT = (((1,), (1,)), ((), ()))


def _params(sem):
    return pltpu.CompilerParams(dimension_semantics=sem, vmem_limit_bytes=VMEM_LIMIT_BYTES)


def _sigmoid(x):
    return 1.0 / (1.0 + jnp.exp(-x))


def _rms(x, g):
    return x * lax.rsqrt(jnp.mean(x * x, axis=-1, keepdims=True) + EPS) * g


def _ln(x, g, b):
    mu = jnp.mean(x, axis=-1, keepdims=True)
    xc = x - mu
    var = jnp.mean(xc * xc, axis=-1, keepdims=True)
    return xc * lax.rsqrt(var + EPS) * g + b


def _bdot(a, b):
    return jnp.dot(a.astype(BF16), b, preferred_element_type=F32)


def _ada_kernel(c_ref, w_ref, b_ref, o_ref):
    c = c_ref[...]
    s = c * _sigmoid(c)
    o_ref[...] = jnp.dot(s, w_ref[...], precision=lax.Precision.HIGHEST,
                         preferred_element_type=F32) + b_ref[...]


def _ada(c, w_ada, b_ada):
    n, d = c.shape
    w6 = w_ada.shape[1]
    bn = d
    return pl.pallas_call(
        _ada_kernel,
        grid=(w6 // bn,),
        in_specs=[pl.BlockSpec((n, d), lambda j: (0, 0)),
                  pl.BlockSpec((d, bn), lambda j: (0, j)),
                  pl.BlockSpec((1, bn), lambda j: (0, j))],
        out_specs=pl.BlockSpec((n, bn), lambda j: (0, j)),
        out_shape=jax.ShapeDtypeStruct((n, w6), F32),
        compiler_params=_params(("arbitrary",)),
        name="ada",
    )(c, w_ada, b_ada.reshape(1, w6))


def _mix_tail(x, mod, z, a_out, sg, wgo_ref, wout_ref, gffn_ref, x1_ref, n2t_ref):
    d = x.shape[-1]
    dc = d // 2
    u = z[:, 2 * dc:3 * dc]
    g_a = z[:, 4 * dc:4 * dc + d]
    g_b = z[:, 4 * dc + d:]
    b_br = _bdot(u * sg, wgo_ref[...])
    m = _sigmoid(g_a) * a_out + _sigmoid(g_b) * b_br
    gt1 = mod[:, 2 * d:3 * d]
    sh2 = mod[:, 3 * d:4 * d]
    sc2 = mod[:, 4 * d:5 * d]
    x1 = x + gt1 * _bdot(m, wout_ref[...])
    x1_ref[...] = x1.reshape(x1_ref.shape)
    n2 = _rms(x1, gffn_ref[...]) * (1.0 + sc2) + sh2
    n2t_ref[...] = n2.T.astype(BF16)


def _mix_kernel(x_ref, mod_ref, gmix_ref, win_ref, wdw_ref, bdw_ref, gcn_ref, bcn_ref, wco_ref,
                gv_ref, bv_ref, ws_ref, bsx_ref, wgo_ref, wout_ref, gffn_ref,
                x1_ref, n2t_ref, hist_ref, aext_ref):
    j = pl.program_id(1)
    ts = x_ref.shape[1]
    d = x_ref.shape[2]
    dc = d // 2
    x = x_ref[0]
    mod = mod_ref[0]
    sh1 = mod[:, 0:d]
    sc1 = mod[:, d:2 * d]
    n = _rms(x, gmix_ref[...]) * (1.0 + sc1) + sh1
    z = _bdot(n, win_ref[...])
    a = z[:, 0:dc] * _sigmoid(z[:, dc:2 * dc])

    @pl.when(j == 0)
    def _():
        aext_ref[0:32, :] = jnp.zeros((32, dc), F32)

    @pl.when(j > 0)
    def _():
        aext_ref[0:32, :] = aext_ref[ts:ts + 32, :]

    aext_ref[32:32 + ts, :] = a
    hist_ref[0, 0] = aext_ref[ts + 2:ts + 32, :]

    acc = jnp.broadcast_to(bdw_ref[...], (ts, dc))
    for k in range(CONV_W):
        acc = acc + wdw_ref[k:k + 1, :] * aext_ref[2 + k:2 + k + ts, :]
    ac = _ln(acc, gcn_ref[...], bcn_ref[...])
    a_out = _bdot(ac * _sigmoid(ac), wco_ref[...])

    vn = _ln(z[:, 3 * dc:4 * dc], gv_ref[...], bv_ref[...])
    row = lax.broadcasted_iota(jnp.int32, (CHUNK, CHUNK), 0)
    col = lax.broadcasted_iota(jnp.int32, (CHUNK, CHUNK), 1)
    causal = col <= row
    first_head = col < (CHUNK // 2)
    wm = [jnp.where(causal, ws_ref[h], 0.0).astype(BF16) for h in range(GMLP_HEADS)]
    vb = vn.astype(BF16)
    chunks = []
    for c in range(ts // CHUNK):
        groups = []
        for p in range(GMLP_HEADS // 2):
            cols = vb[c * CHUNK:(c + 1) * CHUNK, p * CHUNK:(p + 1) * CHUNK]
            r0 = jnp.dot(wm[2 * p], cols, preferred_element_type=F32)
            r1 = jnp.dot(wm[2 * p + 1], cols, preferred_element_type=F32)
            groups.append(jnp.where(first_head, r0, r1))
        chunks.append(jnp.concatenate(groups, axis=1) + bsx_ref[...])
    sg = jnp.concatenate(chunks, axis=0)
    _mix_tail(x, mod, z, a_out, sg, wgo_ref, wout_ref, gffn_ref, x1_ref, n2t_ref)


def _mix1_kernel(x_ref, mod_ref, gmix_ref, win_ref, wdw_ref, bdw_ref, gcn_ref, bcn_ref, wco_ref,
                 gv_ref, bv_ref, ws0_ref, bs0_ref, wgo_ref, wout_ref, gffn_ref, st_ref,
                 x1_ref, n2t_ref, a_ref, vn_ref):
    d = x_ref.shape[1]
    dc = d // 2
    x = x_ref[...]
    mod = mod_ref[...]
    sh1 = mod[:, 0:d]
    sc1 = mod[:, d:2 * d]
    n = _rms(x, gmix_ref[...]) * (1.0 + sc1) + sh1
    z = _bdot(n, win_ref[...])
    a = z[:, 0:dc] * _sigmoid(z[:, dc:2 * dc])
    a_ref[...] = a
    acc = bdw_ref[...] + wdw_ref[CONV_W - 1:CONV_W, :] * a
    for k in range(CONV_W - 1):
        acc = acc + wdw_ref[k:k + 1, :] * st_ref[k]
    ac = _ln(acc, gcn_ref[...], bcn_ref[...])
    a_out = _bdot(ac * _sigmoid(ac), wco_ref[...])
    vn = _ln(z[:, 3 * dc:4 * dc], gv_ref[...], bv_ref[...])
    vn_ref[...] = vn
    sg = ws0_ref[...] * vn + bs0_ref[...]
    _mix_tail(x, mod, z, a_out, sg, wgo_ref, wout_ref, gffn_ref, x1_ref, n2t_ref)


def _const_spec(shape):
    nd = len(shape)
    return pl.BlockSpec(shape, lambda *_: (0,) * nd)


def _mix_weights(p):
    d = p["w_in"].shape[0]
    dc = d // 2
    row = lambda v: v.reshape(1, -1)
    return [row(p["g_mix"]), p["w_in"].astype(BF16), p["w_dw"], row(p["b_dw"]), row(p["g_cn"]),
            row(p["b_cn"]), p["w_conv_out"].astype(BF16), row(p["g_v"]), row(p["b_v"])], \
           [p["w_gmlp_out"].astype(BF16), p["w_out"].astype(BF16), row(p["g_ffn"])]


def _mix_prompt(x, mod, p, ts):
    nb, s, d = x.shape
    dc = d // 2
    pre, post = _mix_weights(p)
    hd = dc // GMLP_HEADS
    bsx = jnp.repeat(p["b_s"].T, hd, axis=1)
    consts = pre + [p["w_s"], bsx] + post
    mod3 = mod.reshape(nb, 1, mod.shape[-1])
    return pl.pallas_call(
        _mix_kernel,
        grid=(nb, s // ts),
        in_specs=[pl.BlockSpec((1, ts, d), lambda b, j: (b, j, 0)),
                  pl.BlockSpec((1, 1, mod.shape[-1]), lambda b, j: (b, 0, 0))]
                 + [_const_spec(c.shape) for c in consts],
        out_specs=[pl.BlockSpec((1, ts, d), lambda b, j: (b, j, 0)),
                   pl.BlockSpec((d, ts), lambda b, j: (0, b * (s // ts) + j)),
                   pl.BlockSpec((1, 1, CONV_W - 1, dc), lambda b, j: (0, b, 0, 0))],
        out_shape=[jax.ShapeDtypeStruct((nb, s, d), F32),
                   jax.ShapeDtypeStruct((d, nb * s), BF16),
                   jax.ShapeDtypeStruct((1, nb, CONV_W - 1, dc), F32)],
        scratch_shapes=[pltpu.VMEM((ts + 32, dc), F32)],
        compiler_params=_params(("arbitrary", "arbitrary")),
        name="mix",
    )(x, mod3, *consts)


def _mix_sample(x, mod, state, p):
    n, d = x.shape
    dc = d // 2
    pre, post = _mix_weights(p)
    hd = dc // GMLP_HEADS
    ws0 = jnp.repeat(p["w_s"][:, 0, 0], hd).reshape(1, dc)
    bs0 = jnp.repeat(p["b_s"][:, 0], hd).reshape(1, dc)
    st = jnp.transpose(state, (1, 0, 2))
    args = [x, mod] + pre + [ws0, bs0] + post + [st]
    return pl.pallas_call(
        _mix1_kernel,
        grid=(1,),
        in_specs=[_const_spec(a.shape) for a in args],
        out_specs=[_const_spec((n, d)), _const_spec((d, n)), _const_spec((n, dc)), _const_spec((n, dc))],
        out_shape=[jax.ShapeDtypeStruct((n, d), F32), jax.ShapeDtypeStruct((d, n), BF16),
                   jax.ShapeDtypeStruct((n, dc), F32), jax.ShapeDtypeStruct((n, dc), F32)],
        compiler_params=_params(("arbitrary",)),
        name="mix1",
    )(*args)


def _top16(s, break_ties):
    nk, tr = s.shape
    iota = lax.broadcasted_iota(jnp.int32, (nk, tr), 0)
    rank = jnp.full((nk, tr), float(TOPK), F32)
    vals = []
    for r in range(TOPK):
        m = jnp.max(s, axis=0, keepdims=True)
        hit = s == m
        if break_ties:
            hit = iota == jnp.min(jnp.where(hit, iota, nk), axis=0, keepdims=True)
        rank = jnp.where(hit, float(r), rank)
        s = jnp.where(hit, NEG_INF, s)
        vals.append(m)
    return jnp.concatenate(vals, axis=0), rank


def _count(mask):
    return jnp.sum(jnp.where(mask, 1.0, 0.0), axis=0, keepdims=True)


def _route_kernel(n2t_ref, wqt_ref, k1_ref, k2_ref, nsel_ref, e1_ref, rb_ref, e2_ref):
    tr = n2t_ref.shape[1]
    n2t = n2t_ref[...]
    k1 = k1_ref[...]
    k2 = k2_ref[...]
    half = k1.shape[1]

    i8 = lax.broadcasted_iota(jnp.int32, (8, tr), 0)
    i16 = lax.broadcasted_iota(jnp.int32, (16, tr), 0)
    ids = jnp.concatenate([i16] + [r * TOPK + i8 for r in range(1, 8)] + [(i8 + 8) * TOPK], axis=0)
    ok = jnp.concatenate([i16 >= 0] + [(r + 1) * (i8 + 1) <= TOPK for r in range(1, 8)] + [i8 >= 0], axis=0)
    big = TOPK * TOPK

    def head(h, carry):
        off = pl.multiple_of(h * 2 * half, 2 * half)
        qt = jnp.dot(wqt_ref[pl.ds(off, 2 * half), :], n2t, preferred_element_type=F32)
        s1 = jnp.dot(k1, qt[0:half].astype(BF16), preferred_element_type=F32)
        s2 = jnp.dot(k2, qt[half:].astype(BF16), preferred_element_type=F32)
        def select(break_ties):
            sc1, rank1 = _top16(s1, break_ties)
            sc2, rank2 = _top16(s2, break_ties)
            cand0 = jnp.concatenate([sc1[0:1] + sc2] + [sc1[r:r + 1] + sc2[0:8] for r in range(1, 8)]
                                    + [sc1[8:16] + sc2[0:1]], axis=0)
            cand0 = jnp.where(ok, cand0, NEG_INF)
            cand = cand0
            sel = jnp.zeros(cand.shape, jnp.bool_)
            for _ in range(TOPK):
                m = jnp.max(cand, axis=0, keepdims=True)
                hit = cand == m
                if break_ties:
                    hit = ids == jnp.min(jnp.where(hit, ids, big), axis=0, keepdims=True)
                sel = jnp.logical_or(sel, hit)
                cand = jnp.where(hit, NEG_INF, cand)
            top = sc1[0:1] + sc2[0:1]
            z = jnp.sum(jnp.where(sel, jnp.exp(cand0 - top), 0.0), axis=0, keepdims=True)
            self32 = jnp.where(sel, 1.0, 0.0)
            counts = [jnp.sum(self32[0:16], axis=0, keepdims=True)]
            counts += [jnp.sum(self32[8 + 8 * r:16 + 8 * r], axis=0, keepdims=True) for r in range(1, 8)]
            tail = self32[72:80]
            nsel = jnp.zeros((N_KEYS, tr), F32)
            for r in range(TOPK):
                n_r = counts[r] if r < 8 else tail[r - 8:r - 7]
                nsel = jnp.where(rank1 == float(r), n_r, nsel)
            e1 = jnp.exp(s1 - sc1[0:1]) / z
            e2 = jnp.exp(s2 - sc2[0:1])
            clean = ((_count(rank1 < float(TOPK)) == float(TOPK)) & (_count(rank2 < float(TOPK)) == float(TOPK))
                     & (_count(sel) == float(TOPK)))
            return (nsel, e1, rank2, e2), clean

        outs, clean = select(False)
        ties = jnp.sum(jnp.where(clean, 0.0, 1.0))
        outs = lax.cond(ties == 0.0, lambda: outs, lambda: select(True)[0])
        nsel_ref[h] = outs[0]
        e1_ref[h] = outs[1]
        rb_ref[h] = outs[2]
        e2_ref[h] = outs[3]
        return carry

    lax.fori_loop(0, PEER_HEADS, head, 0)


def _route(n2t, wqt, k1, k2, tr):
    d, t = n2t.shape
    ospec = pl.BlockSpec((PEER_HEADS, N_KEYS, tr), lambda i: (0, 0, i))
    return pl.pallas_call(
        _route_kernel,
        grid=(t // tr,),
        in_specs=[pl.BlockSpec((d, tr), lambda i: (0, i)),
                  _const_spec(wqt.shape), _const_spec(k1.shape), _const_spec(k2.shape)],
        out_specs=[ospec] * 4,
        out_shape=[jax.ShapeDtypeStruct((PEER_HEADS, N_KEYS, t), F32)] * 4,
        compiler_params=_params(("arbitrary",)),
        name="route",
    )(n2t, wqt, k1, k2)


def _gelu(x):
    return 0.5 * x * (1.0 + lax.erf(x * (2.0 ** -0.5)))


SUB = 256
SLAB = 16


def _experts_kernel(n2t_ref, u_ref, vt_ref, nsel_ref, e1_ref, rb32_ref, e232_ref, x1_ref, gt2_ref, gfin_ref,
                    y_ref, h_scr, a_scr, acc_scr, rb_ref, e2_ref, xt_scr):
    c = pl.program_id(1)
    ec = u_ref.shape[0]
    t = n2t_ref.shape[1]
    n_sub = ec // SUB
    keys_per_sub = SUB // N_KEYS
    lane_tiles = t // 128

    @pl.when(c == 0)
    def _():
        acc_scr[...] = jnp.zeros(acc_scr.shape, F32)
        xt_scr[...] = n2t_ref[...]
        for h in range(PEER_HEADS):
            rb_ref[h] = rb32_ref[h].astype(BF16)
            e2_ref[h] = e232_ref[h].astype(BF16)

    def hidden(s):
        h_scr[s % 2] = jnp.dot(u_ref[s * SUB:(s + 1) * SUB, :], xt_scr[...], preferred_element_type=F32)

    def gate(s):
        for jj in range(keys_per_sub):
            j = s * keys_per_sub + jj
            for lt in range(lane_tiles):
                lanes = slice(lt * 128, (lt + 1) * 128)
                w = [None] * (N_KEYS // SLAB)
                for h in range(PEER_HEADS):
                    n16 = jnp.broadcast_to(nsel_ref[h, j:j + 1, lanes], (SLAB, 128)).astype(BF16)
                    e16 = jnp.broadcast_to(e1_ref[h, j:j + 1, lanes], (SLAB, 128)).astype(BF16)
                    for k in range(N_KEYS // SLAB):
                        keys = slice(k * SLAB, (k + 1) * SLAB)
                        term = jnp.where(rb_ref[h, keys, lanes] < n16, e2_ref[h, keys, lanes], 0.0) * e16
                        w[k] = term if w[k] is None else w[k] + term
                for k in range(N_KEYS // SLAB):
                    rows = slice(jj * N_KEYS + k * SLAB, jj * N_KEYS + (k + 1) * SLAB)
                    act = _gelu(h_scr[s % 2, rows, lanes]).astype(BF16)
                    a_scr[s % 2, rows, lanes] = act * w[k]

    def combine(s):
        acc_scr[...] += jnp.dot(vt_ref[:, s * SUB:(s + 1) * SUB], a_scr[s % 2], preferred_element_type=F32)

    hidden(0)
    for s in range(n_sub):
        if s + 1 < n_sub:
            hidden(s + 1)
        gate(s)
        combine(s)

    @pl.when(c == pl.num_programs(1) - 1)
    def _():
        x2 = x1_ref[...] + gt2_ref[...] * acc_scr[...].T
        y_ref[...] = _rms(x2, gfin_ref[...])


def _experts(n2t, u, vt, route, x1, gt2, gt2_spec, gfin, t, ec):
    d, tot = n2t.shape
    ne = u.shape[0]
    assert ec % SUB == 0 and (ec // N_KEYS) % 8 == 0 and t % 128 == 0
    rspec = pl.BlockSpec((PEER_HEADS, N_KEYS, t), lambda i, c: (0, 0, i))
    cspec = pl.BlockSpec((PEER_HEADS, ec // N_KEYS, t), lambda i, c: (0, c, i))
    return pl.pallas_call(
        _experts_kernel,
        grid=(tot // t, ne // ec),
        in_specs=[pl.BlockSpec((d, t), lambda i, c: (0, i)),
                  pl.BlockSpec((ec, d), lambda i, c: (c, 0)),
                  pl.BlockSpec((None, d, ec), lambda i, c: (c, 0, 0)),
                  cspec, cspec, rspec, rspec,
                  pl.BlockSpec((t, d), lambda i, c: (i, 0)),
                  gt2_spec,
                  pl.BlockSpec((1, d), lambda i, c: (0, 0))],
        out_specs=pl.BlockSpec((t, d), lambda i, c: (i, 0)),
        out_shape=jax.ShapeDtypeStruct((tot, d), F32),
        scratch_shapes=[pltpu.VMEM((2, SUB, t), F32), pltpu.VMEM((2, SUB, t), BF16), pltpu.VMEM((d, t), F32),
                        pltpu.VMEM((PEER_HEADS, N_KEYS, t), BF16), pltpu.VMEM((PEER_HEADS, N_KEYS, t), BF16),
                        pltpu.VMEM((d, t), BF16)],
        compiler_params=_params(("arbitrary", "arbitrary")),
        name="experts",
    )(n2t, u, vt, *route, x1, gt2, gfin)


def _pick(n, cap):
    best = 128
    for b in range(128, cap + 1, 128):
        if n % b == 0:
            best = b
    return best


def kernel(x_prompt, x_sample, state_conv, c_prompt, c_sample, w_ada, b_ada, g_mix, w_in, w_dw, b_dw,
           g_cn, b_cn, w_conv_out, g_v, b_v, w_s, b_s, w_gmlp_out, w_out, g_ffn, w_q, k1, k2,
           u_tab, v_tab, g_final):
    depth = w_ada.shape[0]
    assert depth == 1 and x_sample.shape[1] == 1
    nb, s, d = x_prompt.shape
    ns = x_sample.shape[0]
    assert s % CHUNK == 0 and ns % 128 == 0 and d % 256 == 0
    l = 0
    p = dict(g_mix=g_mix[l], w_in=w_in[l], w_dw=w_dw[l], b_dw=b_dw[l], g_cn=g_cn[l], b_cn=b_cn[l],
             w_conv_out=w_conv_out[l], g_v=g_v[l], b_v=b_v[l], w_s=w_s[l], b_s=b_s[l],
             w_gmlp_out=w_gmlp_out[l], w_out=w_out[l], g_ffn=g_ffn[l])

    mod = _ada(jnp.concatenate([c_prompt, c_sample], axis=0), w_ada[l], b_ada[l])
    mod_p, mod_s = mod[:nb], mod[nb:]

    x1_p, n2t_p, hist_p = _mix_prompt(x_prompt, mod_p, p, _pick(s, 512))
    x1_s, n2t_s, a_s, vn_s = _mix_sample(x_sample.reshape(ns, d), mod_s, state_conv[l], p)

    wqt = w_q[l].T.astype(BF16)
    k1b = k1[l].astype(BF16)
    k2b = k2[l].astype(BF16)
    u = u_tab[l].astype(BF16)
    gfin = g_final.reshape(1, d)
    ec = _pick(u.shape[0], 1024)
    vt = jnp.transpose(v_tab[l].reshape(-1, ec, d), (0, 2, 1)).astype(BF16)

    route_p = _route(n2t_p, wqt, k1b, k2b, _pick(nb * s, 256))
    tp = _pick(s, 512)
    per_seq = s // tp
    gt2_p = mod_p[:, 5 * d:].reshape(nb, 1, d)
    y_p = _experts(n2t_p, u, vt, route_p, x1_p.reshape(nb * s, d), gt2_p,
                   pl.BlockSpec((None, 1, d), lambda i, c: (i // per_seq, 0, 0)), gfin, tp, ec)

    route_s = _route(n2t_s, wqt, k1b, k2b, _pick(ns, 256))
    ts_ = _pick(ns, 512)
    y_s = _experts(n2t_s, u, vt, route_s, x1_s, mod_s[:, 5 * d:],
                   pl.BlockSpec((ts_, d), lambda i, c: (i, 0)), gfin, ts_, ec)

    hist_s = jnp.concatenate([state_conv[l][:, 1:], a_s[:, None, :]], axis=1)[None]
    return (y_p.reshape(nb, s, d), y_s.reshape(ns, 1, d), hist_p, hist_s, vn_s.reshape(1, ns, 1, d // 2))
```

```python
import functools

import jax
import jax.numpy as jnp
from jax import lax
from jax.experimental import pallas as pl
from jax.experimental.pallas import tpu as pltpu

F32 = jnp.float32
BF16 = jnp.bfloat16

CONV_W = 31
GMLP_HEADS = 8
CHUNK = 128
PEER_HEADS = 8
N_KEYS = 128
TOPK = 16
EPS = 1e-6
NEG_INF = float("-inf")

VMEM_LIMIT_BYTES = 56 * 1024 * 1024

_NT = (((1,), (1,)), ((), ()))


def _params(sem):
    return pltpu.CompilerParams(dimension_semantics=sem, vmem_limit_bytes=VMEM_LIMIT_BYTES)


def _sigmoid(x):
    return 1.0 / (1.0 + jnp.exp(-x))


def _rms(x, g):
    return x * lax.rsqrt(jnp.mean(x * x, axis=-1, keepdims=True) + EPS) * g


def _ln(x, g, b):
    mu = jnp.mean(x, axis=-1, keepdims=True)
    xc = x - mu
    var = jnp.mean(xc * xc, axis=-1, keepdims=True)
    return xc * lax.rsqrt(var + EPS) * g + b


def _bdot(a, b):
    return jnp.dot(a.astype(BF16), b, preferred_element_type=F32)


def _ada_kernel(c_ref, w_ref, b_ref, o_ref):
    c = c_ref[...]
    s = c * _sigmoid(c)
    o_ref[...] = jnp.dot(s, w_ref[...], precision=lax.Precision.HIGHEST,
                         preferred_element_type=F32) + b_ref[...]


def _ada(c, w_ada, b_ada):
    n, d = c.shape
    w6 = w_ada.shape[1]
    bn = d
    return pl.pallas_call(
        _ada_kernel,
        grid=(w6 // bn,),
        in_specs=[pl.BlockSpec((n, d), lambda j: (0, 0)),
                  pl.BlockSpec((d, bn), lambda j: (0, j)),
                  pl.BlockSpec((1, bn), lambda j: (0, j))],
        out_specs=pl.BlockSpec((n, bn), lambda j: (0, j)),
        out_shape=jax.ShapeDtypeStruct((n, w6), F32),
        compiler_params=_params(("arbitrary",)),
        name="ada",
    )(c, w_ada, b_ada.reshape(1, w6))


def _mix_tail(x, mod, z, a_out, sg, wgo_ref, wout_ref, gffn_ref, x1_ref, n2t_ref):
    d = x.shape[-1]
    dc = d // 2
    u = z[:, 2 * dc:3 * dc]
    g_a = z[:, 4 * dc:4 * dc + d]
    g_b = z[:, 4 * dc + d:]
    b_br = _bdot(u * sg, wgo_ref[...])
    m = _sigmoid(g_a) * a_out + _sigmoid(g_b) * b_br
    gt1 = mod[:, 2 * d:3 * d]
    sh2 = mod[:, 3 * d:4 * d]
    sc2 = mod[:, 4 * d:5 * d]
    x1 = x + gt1 * _bdot(m, wout_ref[...])
    x1_ref[...] = x1.reshape(x1_ref.shape)
    n2 = _rms(x1, gffn_ref[...]) * (1.0 + sc2) + sh2
    n2t_ref[...] = n2.T.astype(BF16)


def _mix_kernel(x_ref, mod_ref, gmix_ref, win_ref, wdw_ref, bdw_ref, gcn_ref, bcn_ref, wco_ref,
                gv_ref, bv_ref, ws_ref, bsx_ref, wgo_ref, wout_ref, gffn_ref,
                x1_ref, n2t_ref, hist_ref, aext_ref):
    j = pl.program_id(1)
    ts = x_ref.shape[1]
    d = x_ref.shape[2]
    dc = d // 2
    x = x_ref[0]
    mod = mod_ref[0]
    sh1 = mod[:, 0:d]
    sc1 = mod[:, d:2 * d]
    n = _rms(x, gmix_ref[...]) * (1.0 + sc1) + sh1
    z = _bdot(n, win_ref[...])
    a = z[:, 0:dc] * _sigmoid(z[:, dc:2 * dc])

    @pl.when(j == 0)
    def _():
        aext_ref[0:32, :] = jnp.zeros((32, dc), F32)

    @pl.when(j > 0)
    def _():
        aext_ref[0:32, :] = aext_ref[ts:ts + 32, :]

    aext_ref[32:32 + ts, :] = a
    hist_ref[0, 0] = aext_ref[ts + 2:ts + 32, :]

    acc = jnp.broadcast_to(bdw_ref[...], (ts, dc))
    for r in range(8):
        win = ts if r == 0 else ts + 8
        q = None
        for o in range(r, CONV_W + 2, 8):
            if o < 2:
                continue
            term = wdw_ref[o - 2:o - 1, :] * aext_ref[o - r:o - r + win, :]
            q = term if q is None else q + term
        acc = acc + (q if r == 0 else q[r:r + ts])
    ac = _ln(acc, gcn_ref[...], bcn_ref[...])
    a_out = _bdot(ac * _sigmoid(ac), wco_ref[...])

    vn = _ln(z[:, 3 * dc:4 * dc], gv_ref[...], bv_ref[...])
    row = lax.broadcasted_iota(jnp.int32, (CHUNK, CHUNK), 0)
    col = lax.broadcasted_iota(jnp.int32, (CHUNK, CHUNK), 1)
    causal = col <= row
    first_head = col < (CHUNK // 2)
    wm = [jnp.where(causal, ws_ref[h], 0.0).astype(BF16) for h in range(GMLP_HEADS)]
    vb = vn.astype(BF16)
    chunks = []
    for c in range(ts // CHUNK):
        groups = []
        for p in range(GMLP_HEADS // 2):
            cols = vb[c * CHUNK:(c + 1) * CHUNK, p * CHUNK:(p + 1) * CHUNK]
            r0 = jnp.dot(wm[2 * p], cols, preferred_element_type=F32)
            r1 = jnp.dot(wm[2 * p + 1], cols, preferred_element_type=F32)
            groups.append(jnp.where(first_head, r0, r1))
        chunks.append(jnp.concatenate(groups, axis=1) + bsx_ref[...])
    sg = jnp.concatenate(chunks, axis=0)
    _mix_tail(x, mod, z, a_out, sg, wgo_ref, wout_ref, gffn_ref, x1_ref, n2t_ref)


def _mix1_kernel(x_ref, mod_ref, gmix_ref, win_ref, wdw_ref, bdw_ref, gcn_ref, bcn_ref, wco_ref,
                 gv_ref, bv_ref, ws0_ref, bs0_ref, wgo_ref, wout_ref, gffn_ref, st_ref,
                 x1_ref, n2t_ref, a_ref, vn_ref):
    d = x_ref.shape[1]
    dc = d // 2
    x = x_ref[...]
    mod = mod_ref[...]
    sh1 = mod[:, 0:d]
    sc1 = mod[:, d:2 * d]
    n = _rms(x, gmix_ref[...]) * (1.0 + sc1) + sh1
    z = _bdot(n, win_ref[...])
    a = z[:, 0:dc] * _sigmoid(z[:, dc:2 * dc])
    a_ref[...] = a
    acc = bdw_ref[...] + wdw_ref[CONV_W - 1:CONV_W, :] * a
    for k in range(CONV_W - 1):
        acc = acc + wdw_ref[k:k + 1, :] * st_ref[k]
    ac = _ln(acc, gcn_ref[...], bcn_ref[...])
    a_out = _bdot(ac * _sigmoid(ac), wco_ref[...])
    vn = _ln(z[:, 3 * dc:4 * dc], gv_ref[...], bv_ref[...])
    vn_ref[...] = vn
    sg = ws0_ref[...] * vn + bs0_ref[...]
    _mix_tail(x, mod, z, a_out, sg, wgo_ref, wout_ref, gffn_ref, x1_ref, n2t_ref)


def _const_spec(shape):
    nd = len(shape)
    return pl.BlockSpec(shape, lambda *_: (0,) * nd)


def _mix_weights(p):
    d = p["w_in"].shape[0]
    dc = d // 2
    row = lambda v: v.reshape(1, -1)
    return [row(p["g_mix"]), p["w_in"].astype(BF16), p["w_dw"], row(p["b_dw"]), row(p["g_cn"]),
            row(p["b_cn"]), p["w_conv_out"].astype(BF16), row(p["g_v"]), row(p["b_v"])], \
           [p["w_gmlp_out"].astype(BF16), p["w_out"].astype(BF16), row(p["g_ffn"])]


def _mix_prompt(x, mod, p, ts):
    nb, s, d = x.shape
    dc = d // 2
    pre, post = _mix_weights(p)
    hd = dc // GMLP_HEADS
    bsx = jnp.repeat(p["b_s"].T, hd, axis=1)
    consts = pre + [p["w_s"], bsx] + post
    mod3 = mod.reshape(nb, 1, mod.shape[-1])
    return pl.pallas_call(
        _mix_kernel,
        grid=(nb, s // ts),
        in_specs=[pl.BlockSpec((1, ts, d), lambda b, j: (b, j, 0)),
                  pl.BlockSpec((1, 1, mod.shape[-1]), lambda b, j: (b, 0, 0))]
                 + [_const_spec(c.shape) for c in consts],
        out_specs=[pl.BlockSpec((1, ts, d), lambda b, j: (b, j, 0)),
                   pl.BlockSpec((d, ts), lambda b, j: (0, b * (s // ts) + j)),
                   pl.BlockSpec((1, 1, CONV_W - 1, dc), lambda b, j: (0, b, 0, 0))],
        out_shape=[jax.ShapeDtypeStruct((nb, s, d), F32),
                   jax.ShapeDtypeStruct((d, nb * s), BF16),
                   jax.ShapeDtypeStruct((1, nb, CONV_W - 1, dc), F32)],
        scratch_shapes=[pltpu.VMEM((ts + 32, dc), F32)],
        compiler_params=_params(("arbitrary", "arbitrary")),
        name="mix",
    )(x, mod3, *consts)


def _mix_sample(x, mod, state, p):
    n, d = x.shape
    dc = d // 2
    pre, post = _mix_weights(p)
    hd = dc // GMLP_HEADS
    ws0 = jnp.repeat(p["w_s"][:, 0, 0], hd).reshape(1, dc)
    bs0 = jnp.repeat(p["b_s"][:, 0], hd).reshape(1, dc)
    st = jnp.transpose(state, (1, 0, 2))
    args = [x, mod] + pre + [ws0, bs0] + post + [st]
    return pl.pallas_call(
        _mix1_kernel,
        grid=(1,),
        in_specs=[_const_spec(a.shape) for a in args],
        out_specs=[_const_spec((n, d)), _const_spec((d, n)), _const_spec((n, dc)), _const_spec((n, dc))],
        out_shape=[jax.ShapeDtypeStruct((n, d), F32), jax.ShapeDtypeStruct((d, n), BF16),
                   jax.ShapeDtypeStruct((n, dc), F32), jax.ShapeDtypeStruct((n, dc), F32)],
        compiler_params=_params(("arbitrary",)),
        name="mix1",
    )(*args)


def _top16(s, break_ties):
    nk, tr = s.shape
    iota = lax.broadcasted_iota(jnp.int32, (nk, tr), 0)
    rank = jnp.full((nk, tr), float(TOPK), F32)
    vals = []
    for r in range(TOPK):
        m = jnp.max(s, axis=0, keepdims=True)
        hit = s == m
        if break_ties:
            hit = iota == jnp.min(jnp.where(hit, iota, nk), axis=0, keepdims=True)
        rank = jnp.where(hit, float(r), rank)
        s = jnp.where(hit, NEG_INF, s)
        vals.append(m)
    return jnp.concatenate(vals, axis=0), rank


def _count(mask):
    return jnp.sum(jnp.where(mask, 1.0, 0.0), axis=0, keepdims=True)


def _route_kernel(n2t_ref, wqt_ref, k1_ref, k2_ref, nsel_ref, e1_ref, rb_ref, e2_ref):
    tr = n2t_ref.shape[1]
    n2t = n2t_ref[...]
    k1 = k1_ref[...]
    k2 = k2_ref[...]
    half = k1.shape[1]

    i8 = lax.broadcasted_iota(jnp.int32, (8, tr), 0)
    i16 = lax.broadcasted_iota(jnp.int32, (16, tr), 0)
    ids = jnp.concatenate([i16] + [r * TOPK + i8 for r in range(1, 8)] + [(i8 + 8) * TOPK], axis=0)
    ok = jnp.concatenate([i16 >= 0] + [(r + 1) * (i8 + 1) <= TOPK for r in range(1, 8)] + [i8 >= 0], axis=0)
    big = TOPK * TOPK

    def head(h, carry):
        off = pl.multiple_of(h * 2 * half, 2 * half)
        qt = jnp.dot(wqt_ref[pl.ds(off, 2 * half), :], n2t, preferred_element_type=F32)
        s1 = jnp.dot(k1, qt[0:half].astype(BF16), preferred_element_type=F32)
        s2 = jnp.dot(k2, qt[half:].astype(BF16), preferred_element_type=F32)
        def select(break_ties):
            sc1, rank1 = _top16(s1, break_ties)
            sc2, rank2 = _top16(s2, break_ties)
            cand0 = jnp.concatenate([sc1[0:1] + sc2] + [sc1[r:r + 1] + sc2[0:8] for r in range(1, 8)]
                                    + [sc1[8:16] + sc2[0:1]], axis=0)
            cand0 = jnp.where(ok, cand0, NEG_INF)
            cand = cand0
            sel = jnp.zeros(cand.shape, jnp.bool_)
            for _ in range(TOPK):
                m = jnp.max(cand, axis=0, keepdims=True)
                hit = cand == m
                if break_ties:
                    hit = ids == jnp.min(jnp.where(hit, ids, big), axis=0, keepdims=True)
                sel = jnp.logical_or(sel, hit)
                cand = jnp.where(hit, NEG_INF, cand)
            top = sc1[0:1] + sc2[0:1]
            z = jnp.sum(jnp.where(sel, jnp.exp(cand0 - top), 0.0), axis=0, keepdims=True)
            self32 = jnp.where(sel, 1.0, 0.0)
            counts = [jnp.sum(self32[0:16], axis=0, keepdims=True)]
            counts += [jnp.sum(self32[8 + 8 * r:16 + 8 * r], axis=0, keepdims=True) for r in range(1, 8)]
            tail = self32[72:80]
            nsel = jnp.zeros((N_KEYS, tr), F32)
            for r in range(TOPK):
                n_r = counts[r] if r < 8 else tail[r - 8:r - 7]
                nsel = jnp.where(rank1 == float(r), n_r, nsel)
            e1 = jnp.exp(s1 - sc1[0:1]) / z
            e2 = jnp.exp(s2 - sc2[0:1])
            clean = ((_count(rank1 < float(TOPK)) == float(TOPK)) & (_count(rank2 < float(TOPK)) == float(TOPK))
                     & (_count(sel) == float(TOPK)))
            return (nsel, e1, rank2, e2), clean

        outs, clean = select(False)
        ties = jnp.sum(jnp.where(clean, 0.0, 1.0))
        outs = lax.cond(ties == 0.0, lambda: outs, lambda: select(True)[0])
        nsel_ref[h] = outs[0]
        e1_ref[h] = outs[1]
        rb_ref[h] = outs[2]
        e2_ref[h] = outs[3]
        return carry

    lax.fori_loop(0, PEER_HEADS, head, 0)


def _route(n2t, wqt, k1, k2, tr):
    d, t = n2t.shape
    ospec = pl.BlockSpec((PEER_HEADS, N_KEYS, tr), lambda i: (0, 0, i))
    return pl.pallas_call(
        _route_kernel,
        grid=(t // tr,),
        in_specs=[pl.BlockSpec((d, tr), lambda i: (0, i)),
                  _const_spec(wqt.shape), _const_spec(k1.shape), _const_spec(k2.shape)],
        out_specs=[ospec] * 4,
        out_shape=[jax.ShapeDtypeStruct((PEER_HEADS, N_KEYS, t), F32)] * 4,
        compiler_params=_params(("arbitrary",)),
        name="route",
    )(n2t, wqt, k1, k2)


def _gelu(x):
    return 0.5 * x * (1.0 + lax.erf(x * (2.0 ** -0.5)))


SUB = 256
SLAB = 16


def _experts_kernel(n2t_ref, u_ref, vt_ref, nsel_ref, e1_ref, rb32_ref, e232_ref, x1_ref, gt2_ref, gfin_ref,
                    y_ref, h_scr, a_scr, acc_scr, rb_ref, e2_ref, xt_scr):
    c = pl.program_id(1)
    ec = u_ref.shape[0]
    t = n2t_ref.shape[1]
    n_sub = ec // SUB
    keys_per_sub = SUB // N_KEYS
    lane_tiles = t // 128

    @pl.when(c == 0)
    def _():
        acc_scr[...] = jnp.zeros(acc_scr.shape, F32)
        xt_scr[...] = n2t_ref[...]
        for h in range(PEER_HEADS):
            for lt in range(lane_tiles):
                rb_ref[h, lt] = rb32_ref[h, :, lt * 128:(lt + 1) * 128].astype(BF16)
                e2_ref[h, lt] = e232_ref[h, :, lt * 128:(lt + 1) * 128].astype(BF16)

    def hidden(s):
        hid = jnp.dot(u_ref[s * SUB:(s + 1) * SUB, :], xt_scr[...], preferred_element_type=F32)
        for lt in range(lane_tiles):
            h_scr[s % 2, lt] = hid[:, lt * 128:(lt + 1) * 128]

    def gate(s):
        for jj in range(keys_per_sub):
            j = s * keys_per_sub + jj
            for lt in range(lane_tiles):
                lanes = slice(lt * 128, (lt + 1) * 128)
                w = [None] * (N_KEYS // SLAB)
                for h in range(PEER_HEADS):
                    n16 = jnp.broadcast_to(nsel_ref[h, j:j + 1, lanes], (SLAB, 128)).astype(BF16)
                    e16 = jnp.broadcast_to(e1_ref[h, j:j + 1, lanes], (SLAB, 128)).astype(BF16)
                    for k in range(N_KEYS // SLAB):
                        keys = slice(k * SLAB, (k + 1) * SLAB)
                        term = jnp.where(rb_ref[h, lt, keys, :] < n16, e2_ref[h, lt, keys, :], 0.0) * e16
                        w[k] = term if w[k] is None else w[k] + term
                for k in range(N_KEYS // SLAB):
                    rows = slice(jj * N_KEYS + k * SLAB, jj * N_KEYS + (k + 1) * SLAB)
                    act = _gelu(h_scr[s % 2, lt, rows, :]).astype(BF16)
                    a_scr[s * SUB + rows.start:s * SUB + rows.stop, lanes] = act * w[k]

    hidden(0)
    for s in range(n_sub):
        if s + 1 < n_sub:
            hidden(s + 1)
        gate(s)
    acc_scr[...] += jnp.dot(vt_ref[...], a_scr[...], preferred_element_type=F32)

    @pl.when(c == pl.num_programs(1) - 1)
    def _():
        x2 = x1_ref[...] + gt2_ref[...] * acc_scr[...].T
        y_ref[...] = _rms(x2, gfin_ref[...])


def _experts(n2t, u, vt, route, x1, gt2, gt2_spec, gfin, t, ec):
    d, tot = n2t.shape
    ne = u.shape[0]
    assert ec % SUB == 0 and (ec // N_KEYS) % 8 == 0 and t % 128 == 0
    rspec = pl.BlockSpec((PEER_HEADS, N_KEYS, t), lambda i, c: (0, 0, i))
    cspec = pl.BlockSpec((PEER_HEADS, ec // N_KEYS, t), lambda i, c: (0, c, i))
    return pl.pallas_call(
        _experts_kernel,
        grid=(tot // t, ne // ec),
        in_specs=[pl.BlockSpec((d, t), lambda i, c: (0, i)),
                  pl.BlockSpec((ec, d), lambda i, c: (c, 0)),
                  pl.BlockSpec((None, d, ec), lambda i, c: (c, 0, 0)),
                  cspec, cspec, rspec, rspec,
                  pl.BlockSpec((t, d), lambda i, c: (i, 0)),
                  gt2_spec,
                  pl.BlockSpec((1, d), lambda i, c: (0, 0))],
        out_specs=pl.BlockSpec((t, d), lambda i, c: (i, 0)),
        out_shape=jax.ShapeDtypeStruct((tot, d), F32),
        scratch_shapes=[pltpu.VMEM((2, t // 128, SUB, 128), F32), pltpu.VMEM((ec, t), BF16), pltpu.VMEM((d, t), F32),
                        pltpu.VMEM((PEER_HEADS, t // 128, N_KEYS, 128), BF16),
                        pltpu.VMEM((PEER_HEADS, t // 128, N_KEYS, 128), BF16),
                        pltpu.VMEM((d, t), BF16)],
        compiler_params=_params(("arbitrary", "arbitrary")),
        name="experts",
    )(n2t, u, vt, *route, x1, gt2, gfin)


def _pick(n, cap):
    best = 128
    for b in range(128, cap + 1, 128):
        if n % b == 0:
            best = b
    return best


def kernel(x_prompt, x_sample, state_conv, c_prompt, c_sample, w_ada, b_ada, g_mix, w_in, w_dw, b_dw,
           g_cn, b_cn, w_conv_out, g_v, b_v, w_s, b_s, w_gmlp_out, w_out, g_ffn, w_q, k1, k2,
           u_tab, v_tab, g_final):
    depth = w_ada.shape[0]
    assert depth == 1 and x_sample.shape[1] == 1
    nb, s, d = x_prompt.shape
    ns = x_sample.shape[0]
    assert s % CHUNK == 0 and ns % 128 == 0 and d % 256 == 0
    l = 0
    p = dict(g_mix=g_mix[l], w_in=w_in[l], w_dw=w_dw[l], b_dw=b_dw[l], g_cn=g_cn[l], b_cn=b_cn[l],
             w_conv_out=w_conv_out[l], g_v=g_v[l], b_v=b_v[l], w_s=w_s[l], b_s=b_s[l],
             w_gmlp_out=w_gmlp_out[l], w_out=w_out[l], g_ffn=g_ffn[l])

    mod = _ada(jnp.concatenate([c_prompt, c_sample], axis=0), w_ada[l], b_ada[l])
    mod_p, mod_s = mod[:nb], mod[nb:]

    x1_p, n2t_p, hist_p = _mix_prompt(x_prompt, mod_p, p, _pick(s, 512))
    x1_s, n2t_s, a_s, vn_s = _mix_sample(x_sample.reshape(ns, d), mod_s, state_conv[l], p)

    wqt = w_q[l].T.astype(BF16)
    k1b = k1[l].astype(BF16)
    k2b = k2[l].astype(BF16)
    u = u_tab[l].astype(BF16)
    gfin = g_final.reshape(1, d)
    ec = _pick(u.shape[0], 1024)
    vt = jnp.transpose(v_tab[l].reshape(-1, ec, d), (0, 2, 1)).astype(BF16)

    route_p = _route(n2t_p, wqt, k1b, k2b, _pick(nb * s, 256))
    tp = _pick(s, 512)
    per_seq = s // tp
    gt2_p = mod_p[:, 5 * d:].reshape(nb, 1, d)
    y_p = _experts(n2t_p, u, vt, route_p, x1_p.reshape(nb * s, d), gt2_p,
                   pl.BlockSpec((None, 1, d), lambda i, c: (i // per_seq, 0, 0)), gfin, tp, ec)

    route_s = _route(n2t_s, wqt, k1b, k2b, _pick(ns, 256))
    ts_ = _pick(ns, 512)
    y_s = _experts(n2t_s, u, vt, route_s, x1_s, mod_s[:, 5 * d:],
                   pl.BlockSpec((ts_, d), lambda i, c: (i, 0)), gfin, ts_, ec)

    hist_s = jnp.concatenate([state_conv[l][:, 1:], a_s[:, None, :]], axis=1)[None]
    return (y_p.reshape(nb, s, d), y_s.reshape(ns, 1, d), hist_p, hist_s, vn_s.reshape(1, ns, 1, d // 2))
```

```python
import functools

import jax
import jax.numpy as jnp
from jax import lax
from jax.experimental import pallas as pl
from jax.experimental.pallas import tpu as pltpu

F32 = jnp.float32
BF16 = jnp.bfloat16

CONV_W = 31
GMLP_HEADS = 8
CHUNK = 128
PEER_HEADS = 8
N_KEYS = 128
TOPK = 16
EPS = 1e-6
NEG_INF = float("-inf")

VMEM_LIMIT_BYTES = 56 * 1024 * 1024

_NT = (((1,), (1,)), ((), ()))


def _params(sem):
    return pltpu.CompilerParams(dimension_semantics=sem, vmem_limit_bytes=VMEM_LIMIT_BYTES)


def _sigmoid(x):
    return 1.0 / (1.0 + jnp.exp(-x))


def _rms(x, g):
    return x * lax.rsqrt(jnp.mean(x * x, axis=-1, keepdims=True) + EPS) * g


def _ln(x, g, b):
    mu = jnp.mean(x, axis=-1, keepdims=True)
    xc = x - mu
    var = jnp.mean(xc * xc, axis=-1, keepdims=True)
    return xc * lax.rsqrt(var + EPS) * g + b


def _bdot(a, b):
    return jnp.dot(a.astype(BF16), b, preferred_element_type=F32)


def _ada_kernel(c_ref, w_ref, b_ref, o_ref):
    c = c_ref[...]
    s = c * _sigmoid(c)
    o_ref[...] = jnp.dot(s, w_ref[...], precision=lax.Precision.HIGHEST,
                         preferred_element_type=F32) + b_ref[...]


def _ada(c, w_ada, b_ada):
    n, d = c.shape
    w6 = w_ada.shape[1]
    bn = d
    return pl.pallas_call(
        _ada_kernel,
        grid=(w6 // bn,),
        in_specs=[pl.BlockSpec((n, d), lambda j: (0, 0)),
                  pl.BlockSpec((d, bn), lambda j: (0, j)),
                  pl.BlockSpec((1, bn), lambda j: (0, j))],
        out_specs=pl.BlockSpec((n, bn), lambda j: (0, j)),
        out_shape=jax.ShapeDtypeStruct((n, w6), F32),
        compiler_params=_params(("arbitrary",)),
        name="ada",
    )(c, w_ada, b_ada.reshape(1, w6))


def _mix_tail(x, mod, z, a_out, sg, wgo_ref, wout_ref, gffn_ref, x1_ref, n2t_ref):
    d = x.shape[-1]
    dc = d // 2
    u = z[:, 2 * dc:3 * dc]
    g_a = z[:, 4 * dc:4 * dc + d]
    g_b = z[:, 4 * dc + d:]
    b_br = _bdot(u * sg, wgo_ref[...])
    m = _sigmoid(g_a) * a_out + _sigmoid(g_b) * b_br
    gt1 = mod[:, 2 * d:3 * d]
    sh2 = mod[:, 3 * d:4 * d]
    sc2 = mod[:, 4 * d:5 * d]
    x1 = x + gt1 * _bdot(m, wout_ref[...])
    x1_ref[...] = x1.reshape(x1_ref.shape)
    n2 = _rms(x1, gffn_ref[...]) * (1.0 + sc2) + sh2
    n2t_ref[...] = n2.T.astype(BF16)


def _mix_kernel(x_ref, mod_ref, gmix_ref, win_ref, wdw_ref, bdw_ref, gcn_ref, bcn_ref, wco_ref,
                gv_ref, bv_ref, ws_ref, bsx_ref, wgo_ref, wout_ref, gffn_ref,
                x1_ref, n2t_ref, hist_ref, aext_ref):
    j = pl.program_id(1)
    ts = x_ref.shape[1]
    d = x_ref.shape[2]
    dc = d // 2
    x = x_ref[0]
    mod = mod_ref[0]
    sh1 = mod[:, 0:d]
    sc1 = mod[:, d:2 * d]
    n = _rms(x, gmix_ref[...]) * (1.0 + sc1) + sh1
    z = _bdot(n, win_ref[...])
    a = z[:, 0:dc] * _sigmoid(z[:, dc:2 * dc])

    @pl.when(j == 0)
    def _():
        aext_ref[0:32, :] = jnp.zeros((32, dc), F32)

    @pl.when(j > 0)
    def _():
        aext_ref[0:32, :] = aext_ref[ts:ts + 32, :]

    aext_ref[32:32 + ts, :] = a
    hist_ref[0, 0] = aext_ref[ts + 2:ts + 32, :]

    acc = jnp.broadcast_to(bdw_ref[...], (ts, dc))
    for r in range(8):
        win = ts if r == 0 else ts + 8
        q = None
        for o in range(r, CONV_W + 2, 8):
            if o < 2:
                continue
            term = wdw_ref[o - 2:o - 1, :] * aext_ref[o - r:o - r + win, :]
            q = term if q is None else q + term
        acc = acc + (q if r == 0 else q[r:r + ts])
    ac = _ln(acc, gcn_ref[...], bcn_ref[...])
    a_out = _bdot(ac * _sigmoid(ac), wco_ref[...])

    vn = _ln(z[:, 3 * dc:4 * dc], gv_ref[...], bv_ref[...])
    row = lax.broadcasted_iota(jnp.int32, (CHUNK, CHUNK), 0)
    col = lax.broadcasted_iota(jnp.int32, (CHUNK, CHUNK), 1)
    causal = col <= row
    first_head = col < (CHUNK // 2)
    wm = [jnp.where(causal, ws_ref[h], 0.0).astype(BF16) for h in range(GMLP_HEADS)]
    vb = vn.astype(BF16)
    chunks = []
    for c in range(ts // CHUNK):
        groups = []
        for p in range(GMLP_HEADS // 2):
            cols = vb[c * CHUNK:(c + 1) * CHUNK, p * CHUNK:(p + 1) * CHUNK]
            r0 = jnp.dot(wm[2 * p], cols, preferred_element_type=F32)
            r1 = jnp.dot(wm[2 * p + 1], cols, preferred_element_type=F32)
            groups.append(jnp.where(first_head, r0, r1))
        chunks.append(jnp.concatenate(groups, axis=1) + bsx_ref[...])
    sg = jnp.concatenate(chunks, axis=0)
    _mix_tail(x, mod, z, a_out, sg, wgo_ref, wout_ref, gffn_ref, x1_ref, n2t_ref)


def _mix1_kernel(x_ref, mod_ref, gmix_ref, win_ref, wdw_ref, bdw_ref, gcn_ref, bcn_ref, wco_ref,
                 gv_ref, bv_ref, ws0_ref, bs0_ref, wgo_ref, wout_ref, gffn_ref, st_ref,
                 x1_ref, n2t_ref, a_ref, vn_ref):
    d = x_ref.shape[1]
    dc = d // 2
    x = x_ref[...]
    mod = mod_ref[...]
    sh1 = mod[:, 0:d]
    sc1 = mod[:, d:2 * d]
    n = _rms(x, gmix_ref[...]) * (1.0 + sc1) + sh1
    z = _bdot(n, win_ref[...])
    a = z[:, 0:dc] * _sigmoid(z[:, dc:2 * dc])
    a_ref[...] = a
    acc = bdw_ref[...] + wdw_ref[CONV_W - 1:CONV_W, :] * a
    for k in range(CONV_W - 1):
        acc = acc + wdw_ref[k:k + 1, :] * st_ref[k]
    ac = _ln(acc, gcn_ref[...], bcn_ref[...])
    a_out = _bdot(ac * _sigmoid(ac), wco_ref[...])
    vn = _ln(z[:, 3 * dc:4 * dc], gv_ref[...], bv_ref[...])
    vn_ref[...] = vn
    sg = ws0_ref[...] * vn + bs0_ref[...]
    _mix_tail(x, mod, z, a_out, sg, wgo_ref, wout_ref, gffn_ref, x1_ref, n2t_ref)


def _const_spec(shape):
    nd = len(shape)
    return pl.BlockSpec(shape, lambda *_: (0,) * nd)


def _mix_weights(p):
    d = p["w_in"].shape[0]
    dc = d // 2
    row = lambda v: v.reshape(1, -1)
    return [row(p["g_mix"]), p["w_in"].astype(BF16), p["w_dw"], row(p["b_dw"]), row(p["g_cn"]),
            row(p["b_cn"]), p["w_conv_out"].astype(BF16), row(p["g_v"]), row(p["b_v"])], \
           [p["w_gmlp_out"].astype(BF16), p["w_out"].astype(BF16), row(p["g_ffn"])]


def _mix_prompt(x, mod, p, ts):
    nb, s, d = x.shape
    dc = d // 2
    pre, post = _mix_weights(p)
    hd = dc // GMLP_HEADS
    bsx = jnp.repeat(p["b_s"].T, hd, axis=1)
    consts = pre + [p["w_s"], bsx] + post
    mod3 = mod.reshape(nb, 1, mod.shape[-1])
    return pl.pallas_call(
        _mix_kernel,
        grid=(nb, s // ts),
        in_specs=[pl.BlockSpec((1, ts, d), lambda b, j: (b, j, 0)),
                  pl.BlockSpec((1, 1, mod.shape[-1]), lambda b, j: (b, 0, 0))]
                 + [_const_spec(c.shape) for c in consts],
        out_specs=[pl.BlockSpec((1, ts, d), lambda b, j: (b, j, 0)),
                   pl.BlockSpec((d, ts), lambda b, j: (0, b * (s // ts) + j)),
                   pl.BlockSpec((1, 1, CONV_W - 1, dc), lambda b, j: (0, b, 0, 0))],
        out_shape=[jax.ShapeDtypeStruct((nb, s, d), F32),
                   jax.ShapeDtypeStruct((d, nb * s), BF16),
                   jax.ShapeDtypeStruct((1, nb, CONV_W - 1, dc), F32)],
        scratch_shapes=[pltpu.VMEM((ts + 32, dc), F32)],
        compiler_params=_params(("arbitrary", "arbitrary")),
        name="mix",
    )(x, mod3, *consts)


def _mix_sample(x, mod, state, p):
    n, d = x.shape
    dc = d // 2
    pre, post = _mix_weights(p)
    hd = dc // GMLP_HEADS
    ws0 = jnp.repeat(p["w_s"][:, 0, 0], hd).reshape(1, dc)
    bs0 = jnp.repeat(p["b_s"][:, 0], hd).reshape(1, dc)
    st = jnp.transpose(state, (1, 0, 2))
    args = [x, mod] + pre + [ws0, bs0] + post + [st]
    return pl.pallas_call(
        _mix1_kernel,
        grid=(1,),
        in_specs=[_const_spec(a.shape) for a in args],
        out_specs=[_const_spec((n, d)), _const_spec((d, n)), _const_spec((n, dc)), _const_spec((n, dc))],
        out_shape=[jax.ShapeDtypeStruct((n, d), F32), jax.ShapeDtypeStruct((d, n), BF16),
                   jax.ShapeDtypeStruct((n, dc), F32), jax.ShapeDtypeStruct((n, dc), F32)],
        compiler_params=_params(("arbitrary",)),
        name="mix1",
    )(*args)


def _top16(s, break_ties):
    nk, tr = s.shape
    iota = lax.broadcasted_iota(jnp.int32, (nk, tr), 0)
    rank = jnp.full((nk, tr), float(TOPK), F32)
    vals = []
    for r in range(TOPK):
        m = jnp.max(s, axis=0, keepdims=True)
        hit = s == m
        if break_ties:
            hit = iota == jnp.min(jnp.where(hit, iota, nk), axis=0, keepdims=True)
        rank = jnp.where(hit, float(r), rank)
        s = jnp.where(hit, NEG_INF, s)
        vals.append(m)
    return jnp.concatenate(vals, axis=0), rank


def _count(mask):
    return jnp.sum(jnp.where(mask, 1.0, 0.0), axis=0, keepdims=True)


def _route_kernel(n2t_ref, wqt_ref, k1_ref, k2_ref, nsel_ref, e1_ref, rb_ref, e2_ref):
    tr = n2t_ref.shape[1]
    n2t = n2t_ref[...]
    k1 = k1_ref[...]
    k2 = k2_ref[...]
    half = k1.shape[1]

    i8 = lax.broadcasted_iota(jnp.int32, (8, tr), 0)
    i16 = lax.broadcasted_iota(jnp.int32, (16, tr), 0)
    ids = jnp.concatenate([i16] + [r * TOPK + i8 for r in range(1, 8)] + [(i8 + 8) * TOPK], axis=0)
    ok = jnp.concatenate([i16 >= 0] + [(r + 1) * (i8 + 1) <= TOPK for r in range(1, 8)] + [i8 >= 0], axis=0)
    big = TOPK * TOPK

    def head(h, carry):
        off = pl.multiple_of(h * 2 * half, 2 * half)
        qt = jnp.dot(wqt_ref[pl.ds(off, 2 * half), :], n2t, preferred_element_type=F32)
        s1 = jnp.dot(k1, qt[0:half].astype(BF16), preferred_element_type=F32)
        s2 = jnp.dot(k2, qt[half:].astype(BF16), preferred_element_type=F32)
        def select(break_ties):
            sc1, rank1 = _top16(s1, break_ties)
            sc2, rank2 = _top16(s2, break_ties)
            cand0 = jnp.concatenate([sc1[0:1] + sc2] + [sc1[r:r + 1] + sc2[0:8] for r in range(1, 8)]
                                    + [sc1[8:16] + sc2[0:1]], axis=0)
            cand0 = jnp.where(ok, cand0, NEG_INF)
            cand = cand0
            sel = jnp.zeros(cand.shape, jnp.bool_)
            for _ in range(TOPK):
                m = jnp.max(cand, axis=0, keepdims=True)
                hit = cand == m
                if break_ties:
                    hit = ids == jnp.min(jnp.where(hit, ids, big), axis=0, keepdims=True)
                sel = jnp.logical_or(sel, hit)
                cand = jnp.where(hit, NEG_INF, cand)
            top = sc1[0:1] + sc2[0:1]
            z = jnp.sum(jnp.where(sel, jnp.exp(cand0 - top), 0.0), axis=0, keepdims=True)
            self32 = jnp.where(sel, 1.0, 0.0)
            counts = [jnp.sum(self32[0:16], axis=0, keepdims=True)]
            counts += [jnp.sum(self32[8 + 8 * r:16 + 8 * r], axis=0, keepdims=True) for r in range(1, 8)]
            tail = self32[72:80]
            nsel = jnp.zeros((N_KEYS, tr), F32)
            for r in range(TOPK):
                n_r = counts[r] if r < 8 else tail[r - 8:r - 7]
                nsel = jnp.where(rank1 == float(r), n_r, nsel)
            e1 = jnp.exp(s1 - sc1[0:1]) / z
            e2 = jnp.exp(s2 - sc2[0:1])
            clean = ((_count(rank1 < float(TOPK)) == float(TOPK)) & (_count(rank2 < float(TOPK)) == float(TOPK))
                     & (_count(sel) == float(TOPK)))
            return (nsel, e1, rank2, e2), clean

        outs, clean = select(False)
        ties = jnp.sum(jnp.where(clean, 0.0, 1.0))
        outs = lax.cond(ties == 0.0, lambda: outs, lambda: select(True)[0])
        nsel_ref[h] = outs[0]
        e1_ref[h] = outs[1]
        rb_ref[h] = outs[2]
        e2_ref[h] = outs[3]
        return carry

    lax.fori_loop(0, PEER_HEADS, head, 0)


def _route(n2t, wqt, k1, k2, tr):
    d, t = n2t.shape
    ospec = pl.BlockSpec((PEER_HEADS, N_KEYS, tr), lambda i: (0, 0, i))
    return pl.pallas_call(
        _route_kernel,
        grid=(t // tr,),
        in_specs=[pl.BlockSpec((d, tr), lambda i: (0, i)),
                  _const_spec(wqt.shape), _const_spec(k1.shape), _const_spec(k2.shape)],
        out_specs=[ospec] * 4,
        out_shape=[jax.ShapeDtypeStruct((PEER_HEADS, N_KEYS, t), F32)] * 4,
        compiler_params=_params(("arbitrary",)),
        name="route",
    )(n2t, wqt, k1, k2)


def _gelu(x):
    return 0.5 * x * (1.0 + lax.erf(x * (2.0 ** -0.5)))


SUB = 256
SLAB = 16


def _experts_kernel(n2t_ref, u_ref, vt_ref, nsel_ref, e1_ref, rb32_ref, e232_ref, x1_ref, gt2_ref, gfin_ref,
                    y_ref, h_scr, a_scr, acc_scr, re_ref, xt_scr):
    c = pl.program_id(1)
    ec = u_ref.shape[0]
    t = n2t_ref.shape[1]
    n_sub = ec // SUB
    keys_per_sub = SUB // N_KEYS
    lane_tiles = t // 128

    @pl.when(c == 0)
    def _():
        acc_scr[...] = jnp.zeros(acc_scr.shape, F32)
        xt_scr[...] = n2t_ref[...]
        for h in range(PEER_HEADS):
            for lt in range(lane_tiles):
                rb = rb32_ref[h, :, lt * 128:(lt + 1) * 128].astype(BF16)
                e2 = e232_ref[h, :, lt * 128:(lt + 1) * 128].astype(BF16)
                for k in range(N_KEYS // SLAB):
                    re_ref[h, lt, 2 * k * SLAB:(2 * k + 1) * SLAB, :] = rb[k * SLAB:(k + 1) * SLAB]
                    re_ref[h, lt, (2 * k + 1) * SLAB:(2 * k + 2) * SLAB, :] = e2[k * SLAB:(k + 1) * SLAB]

    def hidden(s):
        hid = jnp.dot(u_ref[s * SUB:(s + 1) * SUB, :], xt_scr[...], preferred_element_type=F32)
        for lt in range(lane_tiles):
            h_scr[s % 2, lt] = hid[:, lt * 128:(lt + 1) * 128]

    def gate(s):
        for jj in range(keys_per_sub):
            j = s * keys_per_sub + jj
            for lt in range(lane_tiles):
                lanes = slice(lt * 128, (lt + 1) * 128)
                w = [None] * (N_KEYS // SLAB)
                for h in range(PEER_HEADS):
                    n16 = jnp.broadcast_to(nsel_ref[h, j:j + 1, lanes], (SLAB, 128)).astype(BF16)
                    e16 = jnp.broadcast_to(e1_ref[h, j:j + 1, lanes], (SLAB, 128)).astype(BF16)
                    for k in range(N_KEYS // SLAB):
                        rank = re_ref[h, lt, 2 * k * SLAB:(2 * k + 1) * SLAB, :]
                        e2 = re_ref[h, lt, (2 * k + 1) * SLAB:(2 * k + 2) * SLAB, :]
                        term = jnp.where(rank < n16, e2, 0.0) * e16
                        w[k] = term if w[k] is None else w[k] + term
                for k in range(N_KEYS // SLAB):
                    rows = slice(jj * N_KEYS + k * SLAB, jj * N_KEYS + (k + 1) * SLAB)
                    act = _gelu(h_scr[s % 2, lt, rows, :]).astype(BF16)
                    a_scr[s * SUB + rows.start:s * SUB + rows.stop, lanes] = act * w[k]

    hidden(0)
    for s in range(n_sub):
        if s + 1 < n_sub:
            hidden(s + 1)
        gate(s)
    acc_scr[...] += jnp.dot(vt_ref[...], a_scr[...], preferred_element_type=F32)

    @pl.when(c == pl.num_programs(1) - 1)
    def _():
        x2 = x1_ref[...] + gt2_ref[...] * acc_scr[...].T
        y_ref[...] = _rms(x2, gfin_ref[...])


def _experts(n2t, u, vt, route, x1, gt2, gt2_spec, gfin, t, ec):
    d, tot = n2t.shape
    ne = u.shape[0]
    assert ec % SUB == 0 and (ec // N_KEYS) % 8 == 0 and t % 128 == 0
    rspec = pl.BlockSpec((PEER_HEADS, N_KEYS, t), lambda i, c: (0, 0, i))
    cspec = pl.BlockSpec((PEER_HEADS, ec // N_KEYS, t), lambda i, c: (0, c, i))
    return pl.pallas_call(
        _experts_kernel,
        grid=(tot // t, ne // ec),
        in_specs=[pl.BlockSpec((d, t), lambda i, c: (0, i)),
                  pl.BlockSpec((ec, d), lambda i, c: (c, 0)),
                  pl.BlockSpec((None, d, ec), lambda i, c: (c, 0, 0)),
                  cspec, cspec, rspec, rspec,
                  pl.BlockSpec((t, d), lambda i, c: (i, 0)),
                  gt2_spec,
                  pl.BlockSpec((1, d), lambda i, c: (0, 0))],
        out_specs=pl.BlockSpec((t, d), lambda i, c: (i, 0)),
        out_shape=jax.ShapeDtypeStruct((tot, d), F32),
        scratch_shapes=[pltpu.VMEM((2, t // 128, SUB, 128), F32), pltpu.VMEM((ec, t), BF16), pltpu.VMEM((d, t), F32),
                        pltpu.VMEM((PEER_HEADS, t // 128, 2 * N_KEYS, 128), BF16),
                        pltpu.VMEM((d, t), BF16)],
        compiler_params=_params(("arbitrary", "arbitrary")),
        name="experts",
    )(n2t, u, vt, *route, x1, gt2, gfin)


def _pick(n, cap):
    best = 128
    for b in range(128, cap + 1, 128):
        if n % b == 0:
            best = b
    return best


def kernel(x_prompt, x_sample, state_conv, c_prompt, c_sample, w_ada, b_ada, g_mix, w_in, w_dw, b_dw,
           g_cn, b_cn, w_conv_out, g_v, b_v, w_s, b_s, w_gmlp_out, w_out, g_ffn, w_q, k1, k2,
           u_tab, v_tab, g_final):
    depth = w_ada.shape[0]
    assert depth == 1 and x_sample.shape[1] == 1
    nb, s, d = x_prompt.shape
    ns = x_sample.shape[0]
    assert s % CHUNK == 0 and ns % 128 == 0 and d % 256 == 0
    l = 0
    p = dict(g_mix=g_mix[l], w_in=w_in[l], w_dw=w_dw[l], b_dw=b_dw[l], g_cn=g_cn[l], b_cn=b_cn[l],
             w_conv_out=w_conv_out[l], g_v=g_v[l], b_v=b_v[l], w_s=w_s[l], b_s=b_s[l],
             w_gmlp_out=w_gmlp_out[l], w_out=w_out[l], g_ffn=g_ffn[l])

    mod = _ada(jnp.concatenate([c_prompt, c_sample], axis=0), w_ada[l], b_ada[l])
    mod_p, mod_s = mod[:nb], mod[nb:]

    x1_p, n2t_p, hist_p = _mix_prompt(x_prompt, mod_p, p, _pick(s, 512))
    x1_s, n2t_s, a_s, vn_s = _mix_sample(x_sample.reshape(ns, d), mod_s, state_conv[l], p)

    wqt = w_q[l].T.astype(BF16)
    k1b = k1[l].astype(BF16)
    k2b = k2[l].astype(BF16)
    u = u_tab[l].astype(BF16)
    gfin = g_final.reshape(1, d)
    ec = _pick(u.shape[0], 1024)
    vt = jnp.transpose(v_tab[l].reshape(-1, ec, d), (0, 2, 1)).astype(BF16)

    route_p = _route(n2t_p, wqt, k1b, k2b, _pick(nb * s, 256))
    tp = _pick(s, 512)
    per_seq = s // tp
    gt2_p = mod_p[:, 5 * d:].reshape(nb, 1, d)
    y_p = _experts(n2t_p, u, vt, route_p, x1_p.reshape(nb * s, d), gt2_p,
                   pl.BlockSpec((None, 1, d), lambda i, c: (i // per_seq, 0, 0)), gfin, tp, ec)

    route_s = _route(n2t_s, wqt, k1b, k2b, _pick(ns, 256))
    ts_ = _pick(ns, 512)
    y_s = _experts(n2t_s, u, vt, route_s, x1_s, mod_s[:, 5 * d:],
                   pl.BlockSpec((ts_, d), lambda i, c: (i, 0)), gfin, ts_, ec)

    hist_s = jnp.concatenate([state_conv[l][:, 1:], a_s[:, None, :]], axis=1)[None]
    return (y_p.reshape(nb, s, d), y_s.reshape(ns, 1, d), hist_p, hist_s, vn_s.reshape(1, ns, 1, d // 2))
```

```python
import functools

import jax
import jax.numpy as jnp
from jax import lax
from jax.experimental import pallas as pl
from jax.experimental.pallas import tpu as pltpu

F32 = jnp.float32
BF16 = jnp.bfloat16

CONV_W = 31
GMLP_HEADS = 8
CHUNK = 128
PEER_HEADS = 8
N_KEYS = 128
TOPK = 16
EPS = 1e-6
NEG_INF = float("-inf")

VMEM_LIMIT_BYTES = 56 * 1024 * 1024

_NT = (((1,), (1,)), ((), ()))


def _params(sem):
    return pltpu.CompilerParams(dimension_semantics=sem, vmem_limit_bytes=VMEM_LIMIT_BYTES)


def _sigmoid(x):
    return 1.0 / (1.0 + jnp.exp(-x))


def _rms(x, g):
    return x * lax.rsqrt(jnp.mean(x * x, axis=-1, keepdims=True) + EPS) * g


def _ln(x, g, b):
    mu = jnp.mean(x, axis=-1, keepdims=True)
    xc = x - mu
    var = jnp.mean(xc * xc, axis=-1, keepdims=True)
    return xc * lax.rsqrt(var + EPS) * g + b


def _bdot(a, b):
    return jnp.dot(a.astype(BF16), b, preferred_element_type=F32)


def _ada_kernel(c_ref, w_ref, b_ref, o_ref):
    c = c_ref[...]
    s = c * _sigmoid(c)
    o_ref[...] = jnp.dot(s, w_ref[...], precision=lax.Precision.HIGHEST,
                         preferred_element_type=F32) + b_ref[...]


def _ada(c, w_ada, b_ada):
    n, d = c.shape
    w6 = w_ada.shape[1]
    bn = d
    return pl.pallas_call(
        _ada_kernel,
        grid=(w6 // bn,),
        in_specs=[pl.BlockSpec((n, d), lambda j: (0, 0)),
                  pl.BlockSpec((d, bn), lambda j: (0, j)),
                  pl.BlockSpec((1, bn), lambda j: (0, j))],
        out_specs=pl.BlockSpec((n, bn), lambda j: (0, j)),
        out_shape=jax.ShapeDtypeStruct((n, w6), F32),
        compiler_params=_params(("arbitrary",)),
        name="ada",
    )(c, w_ada, b_ada.reshape(1, w6))


def _mix_tail(x, mod, z, a_out, sg, wgo_ref, wout_ref, gffn_ref, x1_ref, n2t_ref):
    d = x.shape[-1]
    dc = d // 2
    u = z[:, 2 * dc:3 * dc]
    g_a = z[:, 4 * dc:4 * dc + d]
    g_b = z[:, 4 * dc + d:]
    b_br = _bdot(u * sg, wgo_ref[...])
    m = _sigmoid(g_a) * a_out + _sigmoid(g_b) * b_br
    gt1 = mod[:, 2 * d:3 * d]
    sh2 = mod[:, 3 * d:4 * d]
    sc2 = mod[:, 4 * d:5 * d]
    x1 = x + gt1 * _bdot(m, wout_ref[...])
    x1_ref[...] = x1.reshape(x1_ref.shape)
    n2 = _rms(x1, gffn_ref[...]) * (1.0 + sc2) + sh2
    n2t_ref[...] = n2.T.astype(BF16)


def _mix_kernel(x_ref, mod_ref, gmix_ref, win_ref, wdw_ref, bdw_ref, gcn_ref, bcn_ref, wco_ref,
                gv_ref, bv_ref, ws_ref, bsx_ref, wgo_ref, wout_ref, gffn_ref,
                x1_ref, n2t_ref, hist_ref, aext_ref):
    j = pl.program_id(1)
    ts = x_ref.shape[1]
    d = x_ref.shape[2]
    dc = d // 2
    x = x_ref[0]
    mod = mod_ref[0]
    sh1 = mod[:, 0:d]
    sc1 = mod[:, d:2 * d]
    n = _rms(x, gmix_ref[...]) * (1.0 + sc1) + sh1
    z = _bdot(n, win_ref[...])
    a = z[:, 0:dc] * _sigmoid(z[:, dc:2 * dc])

    @pl.when(j == 0)
    def _():
        aext_ref[0:32, :] = jnp.zeros((32, dc), F32)

    @pl.when(j > 0)
    def _():
        aext_ref[0:32, :] = aext_ref[ts:ts + 32, :]

    aext_ref[32:32 + ts, :] = a
    hist_ref[0, 0] = aext_ref[ts + 2:ts + 32, :]

    acc = jnp.broadcast_to(bdw_ref[...], (ts, dc))
    for r in range(8):
        win = ts if r == 0 else ts + 8
        q = None
        for o in range(r, CONV_W + 2, 8):
            if o < 2:
                continue
            term = wdw_ref[o - 2:o - 1, :] * aext_ref[o - r:o - r + win, :]
            q = term if q is None else q + term
        acc = acc + (q if r == 0 else q[r:r + ts])
    ac = _ln(acc, gcn_ref[...], bcn_ref[...])
    a_out = _bdot(ac * _sigmoid(ac), wco_ref[...])

    vn = _ln(z[:, 3 * dc:4 * dc], gv_ref[...], bv_ref[...])
    row = lax.broadcasted_iota(jnp.int32, (CHUNK, CHUNK), 0)
    col = lax.broadcasted_iota(jnp.int32, (CHUNK, CHUNK), 1)
    causal = col <= row
    first_head = col < (CHUNK // 2)
    wm = [jnp.where(causal, ws_ref[h], 0.0).astype(BF16) for h in range(GMLP_HEADS)]
    vb = vn.astype(BF16)
    chunks = []
    for c in range(ts // CHUNK):
        groups = []
        for p in range(GMLP_HEADS // 2):
            cols = vb[c * CHUNK:(c + 1) * CHUNK, p * CHUNK:(p + 1) * CHUNK]
            r0 = jnp.dot(wm[2 * p], cols, preferred_element_type=F32)
            r1 = jnp.dot(wm[2 * p + 1], cols, preferred_element_type=F32)
            groups.append(jnp.where(first_head, r0, r1))
        chunks.append(jnp.concatenate(groups, axis=1) + bsx_ref[...])
    sg = jnp.concatenate(chunks, axis=0)
    _mix_tail(x, mod, z, a_out, sg, wgo_ref, wout_ref, gffn_ref, x1_ref, n2t_ref)


def _mix1_kernel(x_ref, mod_ref, gmix_ref, win_ref, wdw_ref, bdw_ref, gcn_ref, bcn_ref, wco_ref,
                 gv_ref, bv_ref, ws0_ref, bs0_ref, wgo_ref, wout_ref, gffn_ref, st_ref,
                 x1_ref, n2t_ref, a_ref, vn_ref):
    d = x_ref.shape[1]
    dc = d // 2
    x = x_ref[...]
    mod = mod_ref[...]
    sh1 = mod[:, 0:d]
    sc1 = mod[:, d:2 * d]
    n = _rms(x, gmix_ref[...]) * (1.0 + sc1) + sh1
    z = _bdot(n, win_ref[...])
    a = z[:, 0:dc] * _sigmoid(z[:, dc:2 * dc])
    a_ref[...] = a
    acc = bdw_ref[...] + wdw_ref[CONV_W - 1:CONV_W, :] * a
    for k in range(CONV_W - 1):
        acc = acc + wdw_ref[k:k + 1, :] * st_ref[k]
    ac = _ln(acc, gcn_ref[...], bcn_ref[...])
    a_out = _bdot(ac * _sigmoid(ac), wco_ref[...])
    vn = _ln(z[:, 3 * dc:4 * dc], gv_ref[...], bv_ref[...])
    vn_ref[...] = vn
    sg = ws0_ref[...] * vn + bs0_ref[...]
    _mix_tail(x, mod, z, a_out, sg, wgo_ref, wout_ref, gffn_ref, x1_ref, n2t_ref)


def _const_spec(shape):
    nd = len(shape)
    return pl.BlockSpec(shape, lambda *_: (0,) * nd)


def _mix_weights(p):
    d = p["w_in"].shape[0]
    dc = d // 2
    row = lambda v: v.reshape(1, -1)
    return [row(p["g_mix"]), p["w_in"].astype(BF16), p["w_dw"], row(p["b_dw"]), row(p["g_cn"]),
            row(p["b_cn"]), p["w_conv_out"].astype(BF16), row(p["g_v"]), row(p["b_v"])], \
           [p["w_gmlp_out"].astype(BF16), p["w_out"].astype(BF16), row(p["g_ffn"])]


def _mix_prompt(x, mod, p, ts):
    nb, s, d = x.shape
    dc = d // 2
    pre, post = _mix_weights(p)
    hd = dc // GMLP_HEADS
    bsx = jnp.repeat(p["b_s"].T, hd, axis=1)
    consts = pre + [p["w_s"], bsx] + post
    mod3 = mod.reshape(nb, 1, mod.shape[-1])
    return pl.pallas_call(
        _mix_kernel,
        grid=(nb, s // ts),
        in_specs=[pl.BlockSpec((1, ts, d), lambda b, j: (b, j, 0)),
                  pl.BlockSpec((1, 1, mod.shape[-1]), lambda b, j: (b, 0, 0))]
                 + [_const_spec(c.shape) for c in consts],
        out_specs=[pl.BlockSpec((1, ts, d), lambda b, j: (b, j, 0)),
                   pl.BlockSpec((d, ts), lambda b, j: (0, b * (s // ts) + j)),
                   pl.BlockSpec((1, 1, CONV_W - 1, dc), lambda b, j: (0, b, 0, 0))],
        out_shape=[jax.ShapeDtypeStruct((nb, s, d), F32),
                   jax.ShapeDtypeStruct((d, nb * s), BF16),
                   jax.ShapeDtypeStruct((1, nb, CONV_W - 1, dc), F32)],
        scratch_shapes=[pltpu.VMEM((ts + 32, dc), F32)],
        compiler_params=_params(("arbitrary", "arbitrary")),
        name="mix",
    )(x, mod3, *consts)


def _mix_sample(x, mod, state, p):
    n, d = x.shape
    dc = d // 2
    pre, post = _mix_weights(p)
    hd = dc // GMLP_HEADS
    ws0 = jnp.repeat(p["w_s"][:, 0, 0], hd).reshape(1, dc)
    bs0 = jnp.repeat(p["b_s"][:, 0], hd).reshape(1, dc)
    st = jnp.transpose(state, (1, 0, 2))
    args = [x, mod] + pre + [ws0, bs0] + post + [st]
    return pl.pallas_call(
        _mix1_kernel,
        grid=(1,),
        in_specs=[_const_spec(a.shape) for a in args],
        out_specs=[_const_spec((n, d)), _const_spec((d, n)), _const_spec((n, dc)), _const_spec((n, dc))],
        out_shape=[jax.ShapeDtypeStruct((n, d), F32), jax.ShapeDtypeStruct((d, n), BF16),
                   jax.ShapeDtypeStruct((n, dc), F32), jax.ShapeDtypeStruct((n, dc), F32)],
        compiler_params=_params(("arbitrary",)),
        name="mix1",
    )(*args)


def _top16(s, break_ties):
    nk, tr = s.shape
    iota = lax.broadcasted_iota(jnp.int32, (nk, tr), 0)
    rank = jnp.full((nk, tr), float(TOPK), F32)
    vals = []
    for r in range(TOPK):
        m = jnp.max(s, axis=0, keepdims=True)
        hit = s == m
        if break_ties:
            hit = iota == jnp.min(jnp.where(hit, iota, nk), axis=0, keepdims=True)
        rank = jnp.where(hit, float(r), rank)
        s = jnp.where(hit, NEG_INF, s)
        vals.append(m)
    return jnp.concatenate(vals, axis=0), rank


def _count(mask):
    return jnp.sum(jnp.where(mask, 1.0, 0.0), axis=0, keepdims=True)


def _route_kernel(n2t_ref, wqt_ref, k1_ref, k2_ref, nsel_ref, e1_ref, rb_ref, e2_ref):
    tr = n2t_ref.shape[1]
    n2t = n2t_ref[...]
    k1 = k1_ref[...]
    k2 = k2_ref[...]
    half = k1.shape[1]

    i8 = lax.broadcasted_iota(jnp.int32, (8, tr), 0)
    i16 = lax.broadcasted_iota(jnp.int32, (16, tr), 0)
    ids = jnp.concatenate([i16] + [r * TOPK + i8 for r in range(1, 8)] + [(i8 + 8) * TOPK], axis=0)
    ok = jnp.concatenate([i16 >= 0] + [(r + 1) * (i8 + 1) <= TOPK for r in range(1, 8)] + [i8 >= 0], axis=0)
    big = TOPK * TOPK

    def head(h, carry):
        off = pl.multiple_of(h * 2 * half, 2 * half)
        qt = jnp.dot(wqt_ref[pl.ds(off, 2 * half), :], n2t, preferred_element_type=F32)
        s1 = jnp.dot(k1, qt[0:half].astype(BF16), preferred_element_type=F32)
        s2 = jnp.dot(k2, qt[half:].astype(BF16), preferred_element_type=F32)
        def select(break_ties):
            sc1, rank1 = _top16(s1, break_ties)
            sc2, rank2 = _top16(s2, break_ties)
            cand0 = jnp.concatenate([sc1[0:1] + sc2] + [sc1[r:r + 1] + sc2[0:8] for r in range(1, 8)]
                                    + [sc1[8:16] + sc2[0:1]], axis=0)
            cand0 = jnp.where(ok, cand0, NEG_INF)
            cand = cand0
            sel = jnp.zeros(cand.shape, jnp.bool_)
            for _ in range(TOPK):
                m = jnp.max(cand, axis=0, keepdims=True)
                hit = cand == m
                if break_ties:
                    hit = ids == jnp.min(jnp.where(hit, ids, big), axis=0, keepdims=True)
                sel = jnp.logical_or(sel, hit)
                cand = jnp.where(hit, NEG_INF, cand)
            top = sc1[0:1] + sc2[0:1]
            z = jnp.sum(jnp.where(sel, jnp.exp(cand0 - top), 0.0), axis=0, keepdims=True)
            self32 = jnp.where(sel, 1.0, 0.0)
            counts = [jnp.sum(self32[0:16], axis=0, keepdims=True)]
            counts += [jnp.sum(self32[8 + 8 * r:16 + 8 * r], axis=0, keepdims=True) for r in range(1, 8)]
            tail = self32[72:80]
            nsel = jnp.zeros((N_KEYS, tr), F32)
            for r in range(TOPK):
                n_r = counts[r] if r < 8 else tail[r - 8:r - 7]
                nsel = jnp.where(rank1 == float(r), n_r, nsel)
            e1 = jnp.exp(s1 - sc1[0:1]) / z
            e2 = jnp.exp(s2 - sc2[0:1])
            clean = ((_count(rank1 < float(TOPK)) == float(TOPK)) & (_count(rank2 < float(TOPK)) == float(TOPK))
                     & (_count(sel) == float(TOPK)))
            return (nsel, e1, rank2, e2), clean

        outs, clean = select(False)
        ties = jnp.sum(jnp.where(clean, 0.0, 1.0))
        outs = lax.cond(ties == 0.0, lambda: outs, lambda: select(True)[0])
        nsel_ref[h] = outs[0]
        e1_ref[h] = outs[1]
        rb_ref[h] = outs[2]
        e2_ref[h] = outs[3]
        return carry

    lax.fori_loop(0, PEER_HEADS, head, 0)


def _route(n2t, wqt, k1, k2, tr):
    d, t = n2t.shape
    ospec = pl.BlockSpec((PEER_HEADS, N_KEYS, tr), lambda i: (0, 0, i))
    return pl.pallas_call(
        _route_kernel,
        grid=(t // tr,),
        in_specs=[pl.BlockSpec((d, tr), lambda i: (0, i)),
                  _const_spec(wqt.shape), _const_spec(k1.shape), _const_spec(k2.shape)],
        out_specs=[ospec] * 4,
        out_shape=[jax.ShapeDtypeStruct((PEER_HEADS, N_KEYS, t), F32)] * 4,
        compiler_params=_params(("arbitrary",)),
        name="route",
    )(n2t, wqt, k1, k2)


def _gelu(x):
    return 0.5 * x * (1.0 + lax.erf(x * (2.0 ** -0.5)))


SUB = 512
SLAB = 16


def _experts_kernel(n2t_ref, u_ref, vt_ref, nsel_ref, e1_ref, rb32_ref, e232_ref, x1_ref, gt2_ref, gfin_ref,
                    y_ref, h_scr, a_scr, acc_scr, re_ref, xt_scr):
    c = pl.program_id(1)
    ec = u_ref.shape[0]
    t = n2t_ref.shape[1]
    n_sub = ec // SUB
    keys_per_sub = SUB // N_KEYS
    lane_tiles = t // 128

    @pl.when(c == 0)
    def _():
        acc_scr[...] = jnp.zeros(acc_scr.shape, F32)
        xt_scr[...] = n2t_ref[...]
        for h in range(PEER_HEADS):
            for lt in range(lane_tiles):
                rb = rb32_ref[h, :, lt * 128:(lt + 1) * 128].astype(BF16)
                e2 = e232_ref[h, :, lt * 128:(lt + 1) * 128].astype(BF16)
                for k in range(N_KEYS // SLAB):
                    re_ref[h, lt, 2 * k * SLAB:(2 * k + 1) * SLAB, :] = rb[k * SLAB:(k + 1) * SLAB]
                    re_ref[h, lt, (2 * k + 1) * SLAB:(2 * k + 2) * SLAB, :] = e2[k * SLAB:(k + 1) * SLAB]

    def hidden(s):
        hid = jnp.dot(u_ref[s * SUB:(s + 1) * SUB, :], xt_scr[...], preferred_element_type=F32)
        for lt in range(lane_tiles):
            h_scr[s % 2, lt] = hid[:, lt * 128:(lt + 1) * 128]

    def gate(s):
        for jj in range(keys_per_sub):
            j = s * keys_per_sub + jj
            for lt in range(lane_tiles):
                lanes = slice(lt * 128, (lt + 1) * 128)
                w = [None] * (N_KEYS // SLAB)
                for h in range(PEER_HEADS):
                    n16 = jnp.broadcast_to(nsel_ref[h, j:j + 1, lanes], (SLAB, 128)).astype(BF16)
                    e16 = jnp.broadcast_to(e1_ref[h, j:j + 1, lanes], (SLAB, 128)).astype(BF16)
                    for k in range(N_KEYS // SLAB):
                        rank = re_ref[h, lt, 2 * k * SLAB:(2 * k + 1) * SLAB, :]
                        e2 = re_ref[h, lt, (2 * k + 1) * SLAB:(2 * k + 2) * SLAB, :]
                        term = jnp.where(rank < n16, e2, 0.0) * e16
                        w[k] = term if w[k] is None else w[k] + term
                for k in range(N_KEYS // SLAB):
                    rows = slice(jj * N_KEYS + k * SLAB, jj * N_KEYS + (k + 1) * SLAB)
                    act = _gelu(h_scr[s % 2, lt, rows, :]).astype(BF16)
                    a_scr[s * SUB + rows.start:s * SUB + rows.stop, lanes] = act * w[k]

    hidden(0)
    for s in range(n_sub):
        if s + 1 < n_sub:
            hidden(s + 1)
        gate(s)
    acc_scr[...] += jnp.dot(vt_ref[...], a_scr[...], preferred_element_type=F32)

    @pl.when(c == pl.num_programs(1) - 1)
    def _():
        x2 = x1_ref[...] + gt2_ref[...] * acc_scr[...].T
        y_ref[...] = _rms(x2, gfin_ref[...])


def _experts(n2t, u, vt, route, x1, gt2, gt2_spec, gfin, t, ec):
    d, tot = n2t.shape
    ne = u.shape[0]
    assert ec % SUB == 0 and (ec // N_KEYS) % 8 == 0 and t % 128 == 0
    rspec = pl.BlockSpec((PEER_HEADS, N_KEYS, t), lambda i, c: (0, 0, i))
    cspec = pl.BlockSpec((PEER_HEADS, ec // N_KEYS, t), lambda i, c: (0, c, i))
    return pl.pallas_call(
        _experts_kernel,
        grid=(tot // t, ne // ec),
        in_specs=[pl.BlockSpec((d, t), lambda i, c: (0, i)),
                  pl.BlockSpec((ec, d), lambda i, c: (c, 0)),
                  pl.BlockSpec((None, d, ec), lambda i, c: (c, 0, 0)),
                  cspec, cspec, rspec, rspec,
                  pl.BlockSpec((t, d), lambda i, c: (i, 0)),
                  gt2_spec,
                  pl.BlockSpec((1, d), lambda i, c: (0, 0))],
        out_specs=pl.BlockSpec((t, d), lambda i, c: (i, 0)),
        out_shape=jax.ShapeDtypeStruct((tot, d), F32),
        scratch_shapes=[pltpu.VMEM((2, t // 128, SUB, 128), F32), pltpu.VMEM((ec, t), BF16), pltpu.VMEM((d, t), F32),
                        pltpu.VMEM((PEER_HEADS, t // 128, 2 * N_KEYS, 128), BF16),
                        pltpu.VMEM((d, t), BF16)],
        compiler_params=_params(("arbitrary", "arbitrary")),
        name="experts",
    )(n2t, u, vt, *route, x1, gt2, gfin)


def _pick(n, cap):
    best = 128
    for b in range(128, cap + 1, 128):
        if n % b == 0:
            best = b
    return best


def kernel(x_prompt, x_sample, state_conv, c_prompt, c_sample, w_ada, b_ada, g_mix, w_in, w_dw, b_dw,
           g_cn, b_cn, w_conv_out, g_v, b_v, w_s, b_s, w_gmlp_out, w_out, g_ffn, w_q, k1, k2,
           u_tab, v_tab, g_final):
    depth = w_ada.shape[0]
    assert depth == 1 and x_sample.shape[1] == 1
    nb, s, d = x_prompt.shape
    ns = x_sample.shape[0]
    assert s % CHUNK == 0 and ns % 128 == 0 and d % 256 == 0
    l = 0
    p = dict(g_mix=g_mix[l], w_in=w_in[l], w_dw=w_dw[l], b_dw=b_dw[l], g_cn=g_cn[l], b_cn=b_cn[l],
             w_conv_out=w_conv_out[l], g_v=g_v[l], b_v=b_v[l], w_s=w_s[l], b_s=b_s[l],
             w_gmlp_out=w_gmlp_out[l], w_out=w_out[l], g_ffn=g_ffn[l])

    mod = _ada(jnp.concatenate([c_prompt, c_sample], axis=0), w_ada[l], b_ada[l])
    mod_p, mod_s = mod[:nb], mod[nb:]

    x1_p, n2t_p, hist_p = _mix_prompt(x_prompt, mod_p, p, _pick(s, 512))
    x1_s, n2t_s, a_s, vn_s = _mix_sample(x_sample.reshape(ns, d), mod_s, state_conv[l], p)

    wqt = w_q[l].T.astype(BF16)
    k1b = k1[l].astype(BF16)
    k2b = k2[l].astype(BF16)
    u = u_tab[l].astype(BF16)
    gfin = g_final.reshape(1, d)
    ec = _pick(u.shape[0], 1024)
    vt = jnp.transpose(v_tab[l].reshape(-1, ec, d), (0, 2, 1)).astype(BF16)

    route_p = _route(n2t_p, wqt, k1b, k2b, _pick(nb * s, 256))
    tp = _pick(s, 512)
    per_seq = s // tp
    gt2_p = mod_p[:, 5 * d:].reshape(nb, 1, d)
    y_p = _experts(n2t_p, u, vt, route_p, x1_p.reshape(nb * s, d), gt2_p,
                   pl.BlockSpec((None, 1, d), lambda i, c: (i // per_seq, 0, 0)), gfin, tp, ec)

    route_s = _route(n2t_s, wqt, k1b, k2b, _pick(ns, 256))
    ts_ = _pick(ns, 512)
    y_s = _experts(n2t_s, u, vt, route_s, x1_s, mod_s[:, 5 * d:],
                   pl.BlockSpec((ts_, d), lambda i, c: (i, 0)), gfin, ts_, ec)

    hist_s = jnp.concatenate([state_conv[l][:, 1:], a_s[:, None, :]], axis=1)[None]
    return (y_p.reshape(nb, s, d), y_s.reshape(ns, 1, d), hist_p, hist_s, vn_s.reshape(1, ns, 1, d // 2))
```
